```python
import jax, jax.numpy as jnp
from jax import lax
import numpy as np

D_MODEL = 1024
BATCH = 2
SEQ = 8192
DEPTH = 4
DEC_BATCH = 32
DEC_SEQ = 16
PAST_LEN = 1024

CHUNK = 64
N_MIXERS = 2
N_A_LAYERS = (DEPTH + 1) // 2
N_B_LAYERS = DEPTH // 2
CONV_WIDTH = 4
EPS = 1e-6
D_RNN = D_MODEL
LRU_BLOCKS = 4
LRU_BLOCK_W = D_RNN // LRU_BLOCKS
LRU_C = 8.0
SSD_EXPAND = 2
D_INNER = SSD_EXPAND * D_MODEL
SSD_HEAD_DIM = 64
SSD_HEADS = D_INNER // SSD_HEAD_DIM
SSD_GROUPS = 4
SSD_HPG = SSD_HEADS // SSD_GROUPS
SSD_STATE = 128
SSD_GN = SSD_GROUPS * SSD_STATE
SSD_CONV_DIM = D_INNER + 2 * SSD_GN
SSD_IN_DIM = D_INNER + SSD_CONV_DIM + SSD_HEADS
D_FF = -(-8 * D_MODEL // (3 * 256)) * 256

kernel_name = 'hybrid_rglru_ssd_stream_step'


def rmsnorm(x, g):
    xf = x.astype(jnp.float32)
    y = xf * lax.rsqrt(jnp.mean(xf * xf, axis=-1, keepdims=True) + EPS)
    return (y * g.astype(jnp.float32)).astype(x.dtype)


def causal_conv(inp, prev, w, b):
    L = inp.shape[1]
    full = jnp.concatenate([prev.astype(inp.dtype), inp], axis=1)
    out = b + full[:, 0:L] * w[0]
    for k in range(1, CONV_WIDTH):
        out = out + full[:, k:k + L] * w[k]
    return out, full[:, L:]


def _lin_combine(e1, e2):
    a1, b1 = e1
    a2, b2 = e2
    return a1 * a2, a2 * b1 + b2


def rglru_mixer(h, conv_prev, h_prev, w_in, conv_w, conv_b, w_r, b_r, w_i, b_i, lam, w_out):
    bsz, L, _ = h.shape
    proj = h @ w_in
    gate_br, rec_br = proj[..., :D_RNN], proj[..., D_RNN:]
    xc, new_conv = causal_conv(rec_br, conv_prev, conv_w, conv_b)
    xb = xc.reshape(bsz, L, LRU_BLOCKS, LRU_BLOCK_W)
    r = jax.nn.sigmoid(jnp.einsum('blki,kij->blkj', xb, w_r) + b_r).reshape(bsz, L, D_RNN)
    i = jax.nn.sigmoid(jnp.einsum('blki,kij->blkj', xb, w_i) + b_i).reshape(bsz, L, D_RNN)
    log_a = -LRU_C * r.astype(jnp.float32) * jax.nn.softplus(-lam.astype(jnp.float32))
    a = jnp.exp(log_a)
    u = jnp.sqrt(-jnp.expm1(2.0 * log_a)) * (i * xc).astype(jnp.float32)
    a_cum, u_cum = lax.associative_scan(_lin_combine, (a, u), axis=1)
    hs = u_cum + a_cum * h_prev.astype(jnp.float32)[:, None]
    out = (hs.astype(h.dtype) * jax.nn.gelu(gate_br, approximate=True)) @ w_out
    return out, new_conv, hs[:, -1].astype(h.dtype)


def ssd_scan(x, dt, A, Bm, Cm, s0):
    b, L = x.shape[0], x.shape[1]
    q = CHUNK if L % CHUNK == 0 else L
    c = L // q
    G, R, P, N = SSD_GROUPS, SSD_HPG, SSD_HEAD_DIM, SSD_STATE
    dtr = dt.reshape(b, c, q, G, R)
    dA = dtr * A.reshape(G, R)
    xdt = x.reshape(b, c, q, G, R, P) * dtr[..., None]
    Br = Bm.reshape(b, c, q, G, N)
    Cr = Cm.reshape(b, c, q, G, N)
    acs = jnp.cumsum(dA, axis=2)
    seg = acs[:, :, :, None] - acs[:, :, None, :]
    mask = jnp.tril(jnp.ones((q, q), dtype=bool))[:, :, None, None]
    lmat = jnp.exp(jnp.where(mask, seg, -jnp.inf))
    cb = jnp.einsum('bcign,bcjgn->bcijg', Cr, Br)
    y_diag = jnp.einsum('bcijgr,bcjgrp->bcigrp', cb[..., None] * lmat, xdt)
    decay = jnp.exp(acs[:, :, -1:] - acs)
    states = jnp.einsum('bcjgn,bcjgrp->bcgrpn', Br, xdt * decay[..., None])
    blk_decay = jnp.exp(acs[:, :, -1])

    def step(s, inp):
        st, dec = inp
        return s * dec[..., None, None] + st, s

    s_final, s_in = lax.scan(step, s0.reshape(b, G, R, P, N),
                             (jnp.swapaxes(states, 0, 1), jnp.swapaxes(blk_decay, 0, 1)))
    s_in = jnp.swapaxes(s_in, 0, 1)
    y_off = jnp.einsum('bcign,bcgrpn->bcigrp', Cr, s_in) * jnp.exp(acs)[..., None]
    y = (y_diag + y_off).reshape(b, L, SSD_HEADS, P)
    return y, s_final.reshape(b, SSD_HEADS, P, N)


def ssd_mixer(h, conv_prev, s_prev, w_in, conv_w, conv_b, dt_bias, a_log, d_skip, norm_g, w_out):
    bsz, L, _ = h.shape
    proj = h @ w_in
    z = proj[..., :D_INNER]
    xbc = proj[..., D_INNER:D_INNER + SSD_CONV_DIM]
    dt_raw = proj[..., D_INNER + SSD_CONV_DIM:]
    xbc_c, new_conv = causal_conv(xbc, conv_prev, conv_w, conv_b)
    xbc_c = jax.nn.silu(xbc_c).astype(jnp.float32)
    xs = xbc_c[..., :D_INNER].reshape(bsz, L, SSD_HEADS, SSD_HEAD_DIM)
    Bm = xbc_c[..., D_INNER:D_INNER + SSD_GN].reshape(bsz, L, SSD_GROUPS, SSD_STATE)
    Cm = xbc_c[..., D_INNER + SSD_GN:].reshape(bsz, L, SSD_GROUPS, SSD_STATE)
    dt = jax.nn.softplus(dt_raw.astype(jnp.float32) + dt_bias.astype(jnp.float32))
    A = -jnp.exp(a_log.astype(jnp.float32))
    y, s_new = ssd_scan(xs, dt, A, Bm, Cm, s_prev.astype(jnp.float32))
    y = y + d_skip.astype(jnp.float32)[:, None] * xs
    y = y.reshape(bsz, L, D_INNER) * jax.nn.silu(z.astype(jnp.float32))
    yg = y.reshape(bsz, L, SSD_GROUPS, D_INNER // SSD_GROUPS)
    yg = yg * lax.rsqrt(jnp.mean(yg * yg, axis=-1, keepdims=True) + EPS)
    y = (yg.reshape(bsz, L, D_INNER) * norm_g.astype(jnp.float32)).astype(h.dtype)
    return y @ w_out, new_conv, s_new.astype(h.dtype)


def swiglu(h, w_gate, w_up, w_down):
    return (jax.nn.silu(h @ w_gate) * (h @ w_up)) @ w_down


def trunk(x, lru_conv, lru_h, ssd_conv, ssd_s, p):
    new_lc, new_lh, new_sc, new_ss = [], [], [], []
    for layer in range(DEPTH):
        j = layer // N_MIXERS
        hn = rmsnorm(x, p['norm_mix_pre'][layer])
        if layer % N_MIXERS == 0:
            mix, c_new, s_new = rglru_mixer(
                hn, lru_conv[j], lru_h[j], p['lru_w_in'][j], p['lru_conv_w'][j], p['lru_conv_b'][j],
                p['lru_w_r'][j], p['lru_b_r'][j], p['lru_w_i'][j], p['lru_b_i'][j],
                p['lru_lambda'][j], p['lru_w_out'][j])
            new_lc.append(c_new)
            new_lh.append(s_new)
        else:
            mix, c_new, s_new = ssd_mixer(
                hn, ssd_conv[j], ssd_s[j], p['ssd_w_in'][j], p['ssd_conv_w'][j], p['ssd_conv_b'][j],
                p['ssd_dt_bias'][j], p['ssd_a_log'][j], p['ssd_d'][j], p['ssd_norm'][j],
                p['ssd_w_out'][j])
            new_sc.append(c_new)
            new_ss.append(s_new)
        x = x + rmsnorm(mix, p['norm_mix_post'][layer])
        hn = rmsnorm(x, p['norm_ffn_pre'][layer])
        f = swiglu(hn, p['ffn_w_gate'][layer], p['ffn_w_up'][layer], p['ffn_w_down'][layer])
        x = x + rmsnorm(f, p['norm_ffn_post'][layer])
    return x, jnp.stack(new_lc), jnp.stack(new_lh), jnp.stack(new_sc), jnp.stack(new_ss)


def setup_inputs(seed: int = 0) -> dict:
    key = jax.random.key(seed)
    ks = iter(jax.random.split(key, 48))
    f32 = jnp.float32

    def nrm(shape, scale):
        return jax.random.normal(next(ks), shape, f32) * scale

    def gain(shape):
        return 1.0 + 0.05 * jax.random.normal(next(ks), shape, f32)

    NA, NB = N_A_LAYERS, N_B_LAYERS
    a0 = jax.random.uniform(next(ks), (NA, D_RNN), f32, 0.9, 0.999)
    dt0 = jnp.exp(jax.random.uniform(next(ks), (NB, SSD_HEADS), f32, np.log(1e-3), np.log(1e-1)))
    return {
        'x_prompt': nrm((BATCH, SEQ, D_MODEL), 1.0),
        'x_sample': nrm((DEC_BATCH, DEC_SEQ, D_MODEL), 1.0),
        'state_lru_conv': nrm((NA, DEC_BATCH, CONV_WIDTH - 1, D_RNN), 1.0),
        'state_lru_h': nrm((NA, DEC_BATCH, D_RNN), 0.5),
        'state_ssd_conv': nrm((NB, DEC_BATCH, CONV_WIDTH - 1, SSD_CONV_DIM), 1.0),
        'state_ssd': nrm((NB, DEC_BATCH, SSD_HEADS, SSD_HEAD_DIM, SSD_STATE), 0.1),
        'norm_mix_pre': gain((DEPTH, D_MODEL)),
        'norm_mix_post': gain((DEPTH, D_MODEL)),
        'norm_ffn_pre': gain((DEPTH, D_MODEL)),
        'norm_ffn_post': gain((DEPTH, D_MODEL)),
        'lru_w_in': nrm((NA, D_MODEL, 2 * D_RNN), D_MODEL ** -0.5),
        'lru_conv_w': nrm((NA, CONV_WIDTH, D_RNN), CONV_WIDTH ** -0.5),
        'lru_conv_b': nrm((NA, D_RNN), 0.01),
        'lru_w_r': nrm((NA, LRU_BLOCKS, LRU_BLOCK_W, LRU_BLOCK_W), LRU_BLOCK_W ** -0.5),
        'lru_b_r': nrm((NA, LRU_BLOCKS, LRU_BLOCK_W), 0.01),
        'lru_w_i': nrm((NA, LRU_BLOCKS, LRU_BLOCK_W, LRU_BLOCK_W), LRU_BLOCK_W ** -0.5),
        'lru_b_i': nrm((NA, LRU_BLOCKS, LRU_BLOCK_W), 0.01),
        'lru_lambda': jnp.log(a0) - jnp.log1p(-a0),
        'lru_w_out': nrm((NA, D_RNN, D_MODEL), D_RNN ** -0.5),
        'ssd_w_in': nrm((NB, D_MODEL, SSD_IN_DIM), D_MODEL ** -0.5),
        'ssd_conv_w': nrm((NB, CONV_WIDTH, SSD_CONV_DIM), CONV_WIDTH ** -0.5),
        'ssd_conv_b': nrm((NB, SSD_CONV_DIM), 0.01),
        'ssd_dt_bias': dt0 + jnp.log(-jnp.expm1(-dt0)),
        'ssd_a_log': jnp.log(jax.random.uniform(next(ks), (NB, SSD_HEADS), f32, 1.0, 16.0)),
        'ssd_d': 1.0 + 0.1 * jax.random.normal(next(ks), (NB, SSD_HEADS), f32),
        'ssd_norm': gain((NB, D_INNER)),
        'ssd_w_out': nrm((NB, D_INNER, D_MODEL), D_INNER ** -0.5),
        'ffn_w_gate': nrm((DEPTH, D_MODEL, D_FF), D_MODEL ** -0.5),
        'ffn_w_up': nrm((DEPTH, D_MODEL, D_FF), D_MODEL ** -0.5),
        'ffn_w_down': nrm((DEPTH, D_FF, D_MODEL), D_FF ** -0.5),
    }


def reference(x_prompt, x_sample, state_lru_conv, state_lru_h, state_ssd_conv, state_ssd,
              norm_mix_pre, norm_mix_post, norm_ffn_pre, norm_ffn_post,
              lru_w_in, lru_conv_w, lru_conv_b, lru_w_r, lru_b_r, lru_w_i, lru_b_i, lru_lambda, lru_w_out,
              ssd_w_in, ssd_conv_w, ssd_conv_b, ssd_dt_bias, ssd_a_log, ssd_d, ssd_norm, ssd_w_out,
              ffn_w_gate, ffn_w_up, ffn_w_down):
    p = dict(norm_mix_pre=norm_mix_pre, norm_mix_post=norm_mix_post,
             norm_ffn_pre=norm_ffn_pre, norm_ffn_post=norm_ffn_post,
             lru_w_in=lru_w_in, lru_conv_w=lru_conv_w, lru_conv_b=lru_conv_b,
             lru_w_r=lru_w_r, lru_b_r=lru_b_r, lru_w_i=lru_w_i, lru_b_i=lru_b_i,
             lru_lambda=lru_lambda, lru_w_out=lru_w_out,
             ssd_w_in=ssd_w_in, ssd_conv_w=ssd_conv_w, ssd_conv_b=ssd_conv_b,
             ssd_dt_bias=ssd_dt_bias, ssd_a_log=ssd_a_log, ssd_d=ssd_d, ssd_norm=ssd_norm,
             ssd_w_out=ssd_w_out, ffn_w_gate=ffn_w_gate, ffn_w_up=ffn_w_up, ffn_w_down=ffn_w_down)
    bp = x_prompt.shape[0]
    dt_ = x_prompt.dtype
    z_lc = jnp.zeros((N_A_LAYERS, bp, CONV_WIDTH - 1, D_RNN), dt_)
    z_lh = jnp.zeros((N_A_LAYERS, bp, D_RNN), dt_)
    z_sc = jnp.zeros((N_B_LAYERS, bp, CONV_WIDTH - 1, SSD_CONV_DIM), dt_)
    z_ss = jnp.zeros((N_B_LAYERS, bp, SSD_HEADS, SSD_HEAD_DIM, SSD_STATE), dt_)
    y_prompt, p_lc, p_lh, p_sc, p_ss = trunk(x_prompt, z_lc, z_lh, z_sc, z_ss, p)
    y_sample, s_lc, s_lh, s_sc, s_ss = trunk(x_sample, state_lru_conv, state_lru_h,
                                             state_ssd_conv, state_ssd, p)
    return (y_prompt, y_sample, p_lc, p_lh, p_sc, p_ss, s_lc, s_lh, s_sc, s_ss)
```

```python
import functools
import math

import jax
import jax.numpy as jnp
from jax import lax
from jax.experimental import pallas as pl
from jax.experimental.pallas import tpu as pltpu

F32 = jnp.float32
BF16 = jnp.bfloat16

D_MODEL = 1024
DEPTH = 4
CONV_WIDTH = 4
EPS = 1e-6
D_RNN = D_MODEL
LRU_BLOCKS = 4
LRU_BLOCK_W = D_RNN // LRU_BLOCKS
LRU_C = 8.0
D_INNER = 2 * D_MODEL
SSD_HEAD_DIM = 64
SSD_HEADS = D_INNER // SSD_HEAD_DIM
SSD_GROUPS = 4
SSD_HPG = SSD_HEADS // SSD_GROUPS
SSD_STATE = 128
SSD_GN = SSD_GROUPS * SSD_STATE
SSD_CONV_DIM = D_INNER + 2 * SSD_GN
D_FF = -(-8 * D_MODEL // (3 * 256)) * 256

LANES = 128
SUBLANES = 8
CARRY_ROWS = SUBLANES
CARRY_LO = CARRY_ROWS - (CONV_WIDTH - 1)
SSD_Q = 128
GROUP_W = D_INNER // SSD_GROUPS
PAIR_W = 2 * SSD_HEAD_DIM
FF_BLK = 256
PROJ_BLK = 512
PROMPT_T = 512
FFN_TM = 512
VMEM_LIMIT = 60000 * 1024


def _rmsnorm(x, g):
    ms = jnp.mean(x * x, axis=-1, keepdims=True)
    return x * lax.rsqrt(ms + EPS) * g


def _softplus(x):
    return jnp.maximum(x, 0.0) + jnp.log1p(jnp.exp(-jnp.abs(x)))


def _silu(x):
    return x * jax.nn.sigmoid(x)


def _gelu_tanh(x):
    c = math.sqrt(2.0 / math.pi)
    return x * (0.5 * (1.0 + jnp.tanh(c * (x + 0.044715 * (x * x * x)))))


def _dot(a, b):
    return jnp.dot(a, b, preferred_element_type=F32)


def _conv_taps(full_ref, rows, w, b, col):
    out = b + full_ref[CARRY_LO:CARRY_LO + rows, col] * w[0:1]
    for k in range(1, CONV_WIDTH):
        out = out + full_ref[CARRY_LO + k:CARRY_LO + k + rows, col] * w[k:k + 1]
    return out


def _ffn(x, gpre, gpost, wg_ref, wu_ref, wd_ref, act_ref):
    hn = _rmsnorm(x, gpre).astype(BF16)
    for c in range(D_FF // FF_BLK):
        col = slice(c * FF_BLK, (c + 1) * FF_BLK)
        gate = _dot(hn, wg_ref[:, col])
        up = _dot(hn, wu_ref[:, col])
        act_ref[:, col] = (_silu(gate) * up).astype(BF16)
    f = _dot(act_ref[...], wd_ref[...])
    return x + _rmsnorm(f, gpost)


def _ffn_kernel(x_ref, gpre_ref, gpost_ref, wg_ref, wu_ref, wd_ref, o_ref, act_ref):
    o_ref[...] = _ffn(x_ref[...], gpre_ref[...], gpost_ref[...], wg_ref, wu_ref, wd_ref, act_ref)


def _lru_pre(x, gpre, win_ref):
    hn = _rmsnorm(x, gpre).astype(BF16)
    gate = _dot(hn, win_ref[:, :D_RNN])
    rec = _dot(hn, win_ref[:, D_RNN:])
    return _gelu_tanh(gate), rec


def _scan8(a, u):
    rows, ch = a.shape
    a3 = a.reshape(rows // SUBLANES, SUBLANES, ch)
    u3 = u.reshape(rows // SUBLANES, SUBLANES, ch)
    sub = lax.broadcasted_iota(jnp.int32, a3.shape, 1)
    for s in (1, 2, 4):
        m = sub >= s
        a_sh = jnp.where(m, pltpu.roll(a3, s, 1), 1.0)
        u_sh = jnp.where(m, pltpu.roll(u3, s, 1), 0.0)
        u3 = a3 * u_sh + u3
        a3 = a3 * a_sh
    return a3.reshape(rows, ch), u3.reshape(rows, ch)


def _lru_core(rows, full_ref, g_ref, a_ref, u_ref, h_b, cw, cb, wr_ref, br, wi_ref, bi, lam):
    sp = _softplus(-lam)
    for k in range(LRU_BLOCKS):
        col = slice(k * LRU_BLOCK_W, (k + 1) * LRU_BLOCK_W)
        xc = _conv_taps(full_ref, rows, cw[:, col], cb[:, col], col)
        xcb = xc.astype(BF16)
        r = jax.nn.sigmoid(_dot(xcb, wr_ref[k]) + br[:, col])
        i = jax.nn.sigmoid(_dot(xcb, wi_ref[k]) + bi[:, col])
        log_a = (-LRU_C * r) * sp[:, col]
        a = jnp.exp(log_a)
        u = jnp.sqrt(-jnp.tanh(log_a) * (a * a + 1.0)) * (i * xc)
        a8, u8 = _scan8(a, u)
        a_ref[:, col] = a8
        u_ref[:, col] = u8

    def block(j, h_prev):
        r0 = pl.multiple_of(j * SUBLANES, SUBLANES)
        hs = u_ref[pl.ds(r0, SUBLANES), :] + a_ref[pl.ds(r0, SUBLANES), :] * h_prev
        u_ref[pl.ds(r0, SUBLANES), :] = hs
        return jnp.broadcast_to(hs[SUBLANES - 1:SUBLANES, :], hs.shape)

    h_b = lax.fori_loop(0, rows // SUBLANES, block, h_b)
    return u_ref[...] * g_ref[...], h_b


def _lru_post(x, gated, wout_ref, gpost):
    mix = _dot(gated.astype(BF16), wout_ref[...])
    return x + _rmsnorm(mix, gpost)


def _lru_prompt_kernel(x_ref, gpre_ref, gpost_ref, win_ref, cw_ref, cb_ref, wr_ref, br_ref,
                       wi_ref, bi_ref, lam_ref, wout_ref,
                       o_ref, conv_o_ref, h_o_ref,
                       full_ref, g_ref, a_ref, u_ref, h_ref):
    rows = x_ref.shape[0]

    @pl.when(pl.program_id(1) == 0)
    def _():
        full_ref[0:CARRY_ROWS, :] = jnp.zeros((CARRY_ROWS, D_RNN), F32)
        h_ref[...] = jnp.zeros(h_ref.shape, F32)

    x = x_ref[...]
    g, rec = _lru_pre(x, gpre_ref[...], win_ref)
    g_ref[...] = g
    full_ref[CARRY_ROWS:CARRY_ROWS + rows, :] = rec
    gated, h_b = _lru_core(rows, full_ref, g_ref, a_ref, u_ref, h_ref[...], cw_ref[...], cb_ref[...],
                           wr_ref, br_ref[...], wi_ref, bi_ref[...], lam_ref[...])
    h_ref[...] = h_b
    tail = full_ref[rows:rows + CARRY_ROWS, :]
    full_ref[0:CARRY_ROWS, :] = tail
    conv_o_ref[...] = tail
    h_o_ref[...] = h_b[0:1, :]
    o_ref[...] = _lru_post(x, gated, wout_ref, gpost_ref[...])


def _lru_pre_kernel(x_ref, gpre_ref, win_ref, g_o_ref, rec_o_ref):
    g, rec = _lru_pre(x_ref[...], gpre_ref[...], win_ref)
    g_o_ref[...] = g
    rec_o_ref[...] = rec


def _lru_core_sample_kernel(g_ref, rec_ref, conv_ref, h_ref, cw_ref, cb_ref, wr_ref, br_ref,
                            wi_ref, bi_ref, lam_ref,
                            gated_o_ref, conv_o_ref, h_o_ref,
                            full_ref, a_ref, u_ref):
    rows = rec_ref.shape[0]
    full_ref[0:CARRY_ROWS, :] = jnp.zeros((CARRY_ROWS, D_RNN), F32)
    full_ref[CARRY_LO:CARRY_ROWS, :] = conv_ref[...]
    full_ref[CARRY_ROWS:CARRY_ROWS + rows, :] = rec_ref[...]
    h_b = jnp.broadcast_to(h_ref[...], (SUBLANES, D_RNN))
    gated, h_b = _lru_core(rows, full_ref, g_ref, a_ref, u_ref, h_b, cw_ref[...], cb_ref[...],
                           wr_ref, br_ref[...], wi_ref, bi_ref[...], lam_ref[...])
    gated_o_ref[...] = gated
    conv_o_ref[...] = full_ref[rows:rows + CARRY_ROWS, :]
    h_o_ref[...] = h_b[0:1, :]


def _lru_post_ffn_kernel(x_ref, gated_ref, wout_ref, gpost_ref, fpre_ref, fpost_ref,
                         wg_ref, wu_ref, wd_ref, o_ref, act_ref):
    x = _lru_post(x_ref[...], gated_ref[...], wout_ref, gpost_ref[...])
    o_ref[...] = _ffn(x, fpre_ref[...], fpost_ref[...], wg_ref, wu_ref, wd_ref, act_ref)


def _ssd_pre(x, gpre, wzx_ref, wdt_ref, dtb, z_ref, xbc_ref, xbc_row0):
    rows = x.shape[0]
    hn = _rmsnorm(x, gpre).astype(BF16)
    for c in range(D_INNER // PROJ_BLK):
        col = slice(c * PROJ_BLK, (c + 1) * PROJ_BLK)
        z_ref[:, col] = _dot(hn, wzx_ref[:, col])
    for c in range(SSD_CONV_DIM // PROJ_BLK):
        col = slice(c * PROJ_BLK, (c + 1) * PROJ_BLK)
        wcol = slice(D_INNER + c * PROJ_BLK, D_INNER + (c + 1) * PROJ_BLK)
        xbc_ref[xbc_row0:xbc_row0 + rows, col] = _dot(hn, wzx_ref[:, wcol])
    return _softplus(_dot(hn, wdt_ref[...]) + dtb)


def _ssd_conv_silu(full_ref, rows, cw, cb):
    for c in range(SSD_CONV_DIM // PROJ_BLK):
        col = slice(c * PROJ_BLK, (c + 1) * PROJ_BLK)
        out = _conv_taps(full_ref, rows, cw[:, col], cb[:, col], col)
        full_ref[CARRY_ROWS:CARRY_ROWS + rows, col] = _silu(out)


def _cumsum_rows(x):
    rows = x.shape[0]
    row = lax.broadcasted_iota(jnp.int32, x.shape, 0)
    s = 1
    while s < rows:
        x = x + jnp.where(row >= s, pltpu.roll(x, s, 0), 0.0)
        s *= 2
    return x


def _ssd_core(rows, full_ref, dt_ref, a_neg, d_exp, st_ref, y_ref):
    q = SSD_Q
    tril = (lax.broadcasted_iota(jnp.int32, (q, q), 0) >= lax.broadcasted_iota(jnp.int32, (q, q), 1))
    left = lax.broadcasted_iota(jnp.int32, (q, PAIR_W), 1) < SSD_HEAD_DIM
    contract_last = (((1,), (1,)), ((), ()))
    contract_first = (((0,), (0,)), ((), ()))

    def chunk(c, carry):
        r0 = pl.multiple_of(c * q, q)
        rf = pl.multiple_of(c * q + CARRY_ROWS, SUBLANES)
        dtc = dt_ref[pl.ds(r0, q), :]
        acs = _cumsum_rows(dtc * a_neg)
        acs_t = acs.T
        for g in range(SSD_GROUPS):
            gcol = slice(g * GROUP_W, (g + 1) * GROUP_W)
            bcol = slice(D_INNER + g * SSD_STATE, D_INNER + (g + 1) * SSD_STATE)
            ccol = slice(D_INNER + SSD_GN + g * SSD_STATE, D_INNER + SSD_GN + (g + 1) * SSD_STATE)
            b_g = full_ref[pl.ds(rf, q), bcol].astype(BF16)
            c_g = full_ref[pl.ds(rf, q), ccol].astype(BF16)
            cb = lax.dot_general(c_g, b_g, contract_last, preferred_element_type=F32)
            y_off = _dot(c_g, st_ref[:, gcol].astype(BF16))
            xdd = []
            dec = []
            for pr in range(SSD_HPG // 2):
                h1 = g * SSD_HPG + 2 * pr
                h2 = h1 + 1
                pcol = slice(h1 * SSD_HEAD_DIM, h1 * SSD_HEAD_DIM + PAIR_W)
                a1 = jnp.broadcast_to(acs[:, h1:h1 + 1], (q, q))
                a2 = jnp.broadcast_to(acs[:, h2:h2 + 1], (q, q))
                l1 = jnp.exp(jnp.where(tril, a1 - acs_t[h1:h1 + 1, :], -jnp.inf))
                l2 = jnp.exp(jnp.where(tril, a2 - acs_t[h2:h2 + 1, :], -jnp.inf))
                m = jnp.concatenate([cb * l1, cb * l2], axis=1).astype(BF16)
                a_pair = jnp.where(left, a1, a2)
                dt_pair = jnp.where(left, jnp.broadcast_to(dtc[:, h1:h1 + 1], (q, PAIR_W)),
                                    jnp.broadcast_to(dtc[:, h2:h2 + 1], (q, PAIR_W)))
                xs = full_ref[pl.ds(rf, q), pcol]
                xdt = xs * dt_pair
                rhs = jnp.concatenate([jnp.where(left, xdt, 0.0), jnp.where(left, 0.0, xdt)],
                                      axis=0).astype(BF16)
                y_diag = _dot(m, rhs)
                a_last = a_pair[q - 1:q, :]
                xdd.append((xdt * jnp.exp(a_last - a_pair)).astype(BF16))
                dec.append(jnp.exp(a_last))
                ocol = slice(pr * PAIR_W, (pr + 1) * PAIR_W)
                y_ref[pl.ds(r0, q), pcol] = (y_diag + y_off[:, ocol] * jnp.exp(a_pair)
                                             + d_exp[:, pcol] * xs)
            states = lax.dot_general(b_g, jnp.concatenate(xdd, axis=1), contract_first,
                                     preferred_element_type=F32)
            st_ref[:, gcol] = st_ref[:, gcol] * jnp.concatenate(dec, axis=1) + states
        return carry

    lax.fori_loop(0, rows // q, chunk, 0)


def _ssd_post(x, y_ref, z_ref, normg, wout_ref, gpost, ybf_ref):
    for g in range(SSD_GROUPS):
        col = slice(g * GROUP_W, (g + 1) * GROUP_W)
        yg = y_ref[:, col] * _silu(z_ref[:, col])
        ms = jnp.mean(yg * yg, axis=-1, keepdims=True)
        ybf_ref[:, col] = (yg * lax.rsqrt(ms + EPS) * normg[:, col]).astype(BF16)
    mix = _dot(ybf_ref[...], wout_ref[...])
    return x + _rmsnorm(mix, gpost)


def _ssd_prompt_kernel(x_ref, gpre_ref, gpost_ref, wzx_ref, wdt_ref, dtb_ref, cw_ref, cb_ref,
                       alog_ref, dexp_ref, normg_ref, wout_ref,
                       o_ref, conv_o_ref, st_o_ref,
                       full_ref, z_ref, dt_ref, y_ref, ybf_ref, st_ref):
    rows = x_ref.shape[0]

    @pl.when(pl.program_id(1) == 0)
    def _():
        full_ref[0:CARRY_ROWS, :] = jnp.zeros((CARRY_ROWS, SSD_CONV_DIM), F32)
        st_ref[...] = jnp.zeros(st_ref.shape, F32)

    x = x_ref[...]
    dt_ref[...] = _ssd_pre(x, gpre_ref[...], wzx_ref, wdt_ref, dtb_ref[...], z_ref, full_ref, CARRY_ROWS)
    tail = full_ref[rows:rows + CARRY_ROWS, :]
    _ssd_conv_silu(full_ref, rows, cw_ref[...], cb_ref[...])
    full_ref[0:CARRY_ROWS, :] = tail
    conv_o_ref[...] = tail
    _ssd_core(rows, full_ref, dt_ref, -jnp.exp(alog_ref[...]), dexp_ref[...], st_ref, y_ref)

    @pl.when(pl.program_id(1) == pl.num_programs(1) - 1)
    def _():
        st_o_ref[...] = st_ref[...].T

    o_ref[...] = _ssd_post(x, y_ref, z_ref, normg_ref[...], wout_ref, gpost_ref[...], ybf_ref)


def _ssd_pre_kernel(x_ref, gpre_ref, wzx_ref, wdt_ref, dtb_ref, z_o_ref, xbc_o_ref, dt_o_ref):
    dt_o_ref[...] = _ssd_pre(x_ref[...], gpre_ref[...], wzx_ref, wdt_ref, dtb_ref[...],
                             z_o_ref, xbc_o_ref, 0)


def _ssd_core_sample_kernel(xbc_ref, dt_in_ref, conv_ref, st_in_ref, cw_ref, cb_ref, alog_ref, dexp_ref,
                            y_o_ref, conv_o_ref, st_o_ref,
                            full_ref, dt_ref, y_ref, st_ref):
    rows = xbc_ref.shape[0]
    full_ref[...] = jnp.zeros(full_ref.shape, F32)
    full_ref[CARRY_LO:CARRY_ROWS, :] = conv_ref[...]
    full_ref[CARRY_ROWS:CARRY_ROWS + rows, :] = xbc_ref[...]
    dt_ref[...] = jnp.zeros(dt_ref.shape, F32)
    dt_ref[0:rows, :] = dt_in_ref[...]
    conv_o_ref[...] = full_ref[rows:rows + CARRY_ROWS, :]
    _ssd_conv_silu(full_ref, SSD_Q, cw_ref[...], cb_ref[...])
    st_ref[...] = st_in_ref[...].T
    _ssd_core(SSD_Q, full_ref, dt_ref, -jnp.exp(alog_ref[...]), dexp_ref[...], st_ref, y_ref)
    y_o_ref[...] = y_ref[0:rows, :]
    st_o_ref[...] = st_ref[...].T


def _ssd_post_ffn_kernel(x_ref, y_ref, z_ref, normg_ref, wout_ref, gpost_ref, fpre_ref, fpost_ref,
                         wg_ref, wu_ref, wd_ref, o_ref, ybf_ref, act_ref):
    x = _ssd_post(x_ref[...], y_ref, z_ref, normg_ref[...], wout_ref, gpost_ref[...], ybf_ref)
    o_ref[...] = _ffn(x, fpre_ref[...], fpost_ref[...], wg_ref, wu_ref, wd_ref, act_ref)


def _whole(shape):
    nd = len(shape)
    return pl.BlockSpec(shape, lambda *_: (0,) * nd, pipeline_mode=pl.Buffered(1))


def _params(n_axes):
    return pltpu.CompilerParams(dimension_semantics=("arbitrary",) * n_axes,
                                vmem_limit_bytes=VMEM_LIMIT)


def _ffn_call(x2d, fp):
    n = x2d.shape[0]
    tm = min(FFN_TM, n)
    consts = (fp['gpre'], fp['gpost'], fp['wg'], fp['wu'], fp['wd'])
    return pl.pallas_call(
        _ffn_kernel,
        grid=(n // tm,),
        in_specs=[pl.BlockSpec((tm, D_MODEL), lambda i: (i, 0))] + [_whole(c.shape) for c in consts],
        out_specs=pl.BlockSpec((tm, D_MODEL), lambda i: (i, 0)),
        out_shape=jax.ShapeDtypeStruct((n, D_MODEL), F32),
        scratch_shapes=[pltpu.VMEM((tm, D_FF), BF16)],
        compiler_params=_params(1),
        name="ffn",
    )(x2d, *consts)


def _lru_prompt_call(x, lp):
    bsz, seq, _ = x.shape
    t = PROMPT_T
    consts = (lp['gpre'], lp['gpost'], lp['win'], lp['cw'], lp['cb'], lp['wr'], lp['br'],
              lp['wi'], lp['bi'], lp['lam'], lp['wout'])
    return pl.pallas_call(
        _lru_prompt_kernel,
        grid=(bsz, seq // t),
        in_specs=[pl.BlockSpec((None, t, D_MODEL), lambda b, i: (b, i, 0))]
        + [_whole(c.shape) for c in consts],
        out_specs=[pl.BlockSpec((None, t, D_MODEL), lambda b, i: (b, i, 0)),
                   pl.BlockSpec((None, CARRY_ROWS, D_RNN), lambda b, i: (b, 0, 0)),
                   pl.BlockSpec((None, 1, D_RNN), lambda b, i: (b, 0, 0))],
        out_shape=[jax.ShapeDtypeStruct((bsz, seq, D_MODEL), F32),
                   jax.ShapeDtypeStruct((bsz, CARRY_ROWS, D_RNN), F32),
                   jax.ShapeDtypeStruct((bsz, 1, D_RNN), F32)],
        scratch_shapes=[pltpu.VMEM((t + CARRY_ROWS, D_RNN), F32),
                        pltpu.VMEM((t, D_RNN), F32),
                        pltpu.VMEM((t, D_RNN), F32),
                        pltpu.VMEM((t, D_RNN), F32),
                        pltpu.VMEM((SUBLANES, D_RNN), F32)],
        compiler_params=_params(2),
        name="lru_prompt",
    )(x, *consts)


def _ssd_prompt_call(x, sp):
    bsz, seq, _ = x.shape
    t = PROMPT_T
    consts = (sp['gpre'], sp['gpost'], sp['wzx'], sp['wdt'], sp['dtb'], sp['cw'], sp['cb'],
              sp['alog'], sp['dexp'], sp['normg'], sp['wout'])
    return pl.pallas_call(
        _ssd_prompt_kernel,
        grid=(bsz, seq // t),
        in_specs=[pl.BlockSpec((None, t, D_MODEL), lambda b, i: (b, i, 0))]
        + [_whole(c.shape) for c in consts],
        out_specs=[pl.BlockSpec((None, t, D_MODEL), lambda b, i: (b, i, 0)),
                   pl.BlockSpec((None, CARRY_ROWS, SSD_CONV_DIM), lambda b, i: (b, 0, 0)),
                   pl.BlockSpec((None, D_INNER, SSD_STATE), lambda b, i: (b, 0, 0))],
        out_shape=[jax.ShapeDtypeStruct((bsz, seq, D_MODEL), F32),
                   jax.ShapeDtypeStruct((bsz, CARRY_ROWS, SSD_CONV_DIM), F32),
                   jax.ShapeDtypeStruct((bsz, D_INNER, SSD_STATE), F32)],
        scratch_shapes=[pltpu.VMEM((t + CARRY_ROWS, SSD_CONV_DIM), F32),
                        pltpu.VMEM((t, D_INNER), F32),
                        pltpu.VMEM((t, LANES), F32),
                        pltpu.VMEM((t, D_INNER), F32),
                        pltpu.VMEM((t, D_INNER), BF16),
                        pltpu.VMEM((SSD_STATE, D_INNER), F32)],
        compiler_params=_params(2),
        name="ssd_prompt",
    )(x, *consts)


def _dense_call(kernel, name, ins, out_widths, scratch):
    n = ins[0].shape[0]
    return pl.pallas_call(
        kernel,
        out_shape=[jax.ShapeDtypeStruct((n, w), F32) for w in out_widths],
        scratch_shapes=scratch,
        compiler_params=pltpu.CompilerParams(vmem_limit_bytes=VMEM_LIMIT),
        name=name,
    )(*ins)


def _lru_core_sample_call(g, rec, conv, h, lp):
    bsz, seq, _ = g.shape
    consts = (lp['cw'], lp['cb'], lp['wr'], lp['br'], lp['wi'], lp['bi'], lp['lam'])

    def per_seq(rows, width):
        return pl.BlockSpec((None, rows, width), lambda b: (b, 0, 0))

    return pl.pallas_call(
        _lru_core_sample_kernel,
        grid=(bsz,),
        in_specs=[per_seq(seq, D_RNN), per_seq(seq, D_RNN), per_seq(CONV_WIDTH - 1, D_RNN),
                  per_seq(1, D_RNN)] + [_whole(c.shape) for c in consts],
        out_specs=[per_seq(seq, D_RNN), per_seq(CARRY_ROWS, D_RNN), per_seq(1, D_RNN)],
        out_shape=[jax.ShapeDtypeStruct((bsz, seq, D_RNN), F32),
                   jax.ShapeDtypeStruct((bsz, CARRY_ROWS, D_RNN), F32),
                   jax.ShapeDtypeStruct((bsz, 1, D_RNN), F32)],
        scratch_shapes=[pltpu.VMEM((seq + CARRY_ROWS, D_RNN), F32),
                        pltpu.VMEM((seq, D_RNN), F32),
                        pltpu.VMEM((seq, D_RNN), F32)],
        compiler_params=_params(1),
        name="lru_core_sample",
    )(g, rec, conv, h, *consts)


def _ssd_core_sample_call(xbc, dt, conv, st, sp):
    bsz, seq, _ = xbc.shape
    consts = (sp['cw'], sp['cb'], sp['alog'], sp['dexp'])

    def per_seq(rows, width):
        return pl.BlockSpec((None, rows, width), lambda b: (b, 0, 0))

    return pl.pallas_call(
        _ssd_core_sample_kernel,
        grid=(bsz,),
        in_specs=[per_seq(seq, SSD_CONV_DIM), per_seq(seq, LANES), per_seq(CONV_WIDTH - 1, SSD_CONV_DIM),
                  per_seq(D_INNER, SSD_STATE)] + [_whole(c.shape) for c in consts],
        out_specs=[per_seq(seq, D_INNER), per_seq(CARRY_ROWS, SSD_CONV_DIM), per_seq(D_INNER, SSD_STATE)],
        out_shape=[jax.ShapeDtypeStruct((bsz, seq, D_INNER), F32),
                   jax.ShapeDtypeStruct((bsz, CARRY_ROWS, SSD_CONV_DIM), F32),
                   jax.ShapeDtypeStruct((bsz, D_INNER, SSD_STATE), F32)],
        scratch_shapes=[pltpu.VMEM((SSD_Q + CARRY_ROWS, SSD_CONV_DIM), F32),
                        pltpu.VMEM((SSD_Q, LANES), F32),
                        pltpu.VMEM((SSD_Q, D_INNER), F32),
                        pltpu.VMEM((SSD_STATE, D_INNER), F32)],
        compiler_params=_params(1),
        name="ssd_core_sample",
    )(xbc, dt, conv, st, *consts)


def _row(v):
    return v.reshape(1, -1).astype(F32)


def _pad_lanes(v2d):
    return jnp.pad(v2d, ((0, 0), (0, LANES - v2d.shape[-1])))


def kernel(x_prompt, x_sample, state_lru_conv, state_lru_h, state_ssd_conv, state_ssd, norm_mix_pre, norm_mix_post, norm_ffn_pre, norm_ffn_post, lru_w_in, lru_conv_w, lru_conv_b, lru_w_r, lru_b_r, lru_w_i, lru_b_i, lru_lambda, lru_w_out, ssd_w_in, ssd_conv_w, ssd_conv_b, ssd_dt_bias, ssd_a_log, ssd_d, ssd_norm, ssd_w_out, ffn_w_gate, ffn_w_up, ffn_w_down):
    n_a = lru_w_in.shape[0]
    n_b = ssd_w_in.shape[0]
    lru_p = []
    for j in range(n_a):
        layer = 2 * j
        lru_p.append(dict(
            gpre=_row(norm_mix_pre[layer]), gpost=_row(norm_mix_post[layer]),
            win=lru_w_in[j].astype(BF16), cw=lru_conv_w[j], cb=_row(lru_conv_b[j]),
            wr=lru_w_r[j].astype(BF16), br=_row(lru_b_r[j]),
            wi=lru_w_i[j].astype(BF16), bi=_row(lru_b_i[j]),
            lam=_row(lru_lambda[j]), wout=lru_w_out[j].astype(BF16)))
    ssd_p = []
    for j in range(n_b):
        layer = 2 * j + 1
        w_in = ssd_w_in[j]
        ssd_p.append(dict(
            gpre=_row(norm_mix_pre[layer]), gpost=_row(norm_mix_post[layer]),
            wzx=w_in[:, :D_INNER + SSD_CONV_DIM].astype(BF16),
            wdt=_pad_lanes(w_in[:, D_INNER + SSD_CONV_DIM:]).astype(BF16),
            dtb=_pad_lanes(_row(ssd_dt_bias[j])),
            cw=ssd_conv_w[j], cb=_row(ssd_conv_b[j]),
            alog=_pad_lanes(_row(ssd_a_log[j])),
            dexp=_row(jnp.repeat(ssd_d[j], SSD_HEAD_DIM)),
            normg=_row(ssd_norm[j]), wout=ssd_w_out[j].astype(BF16)))
    ffn_p = []
    for layer in range(DEPTH):
        ffn_p.append(dict(
            gpre=_row(norm_ffn_pre[layer]), gpost=_row(norm_ffn_post[layer]),
            wg=ffn_w_gate[layer].astype(BF16), wu=ffn_w_up[layer].astype(BF16),
            wd=ffn_w_down[layer].astype(BF16)))

    bp, seq, _ = x_prompt.shape
    x = x_prompt
    p_lc, p_lh, p_sc, p_ss = [], [], [], []
    for layer in range(DEPTH):
        j = layer // 2
        if layer % 2 == 0:
            x, conv, h = _lru_prompt_call(x, lru_p[j])
            p_lc.append(conv[:, CARRY_LO:])
            p_lh.append(h[:, 0])
        else:
            x, conv, st = _ssd_prompt_call(x, ssd_p[j])
            p_sc.append(conv[:, CARRY_LO:])
            p_ss.append(st.reshape(bp, SSD_HEADS, SSD_HEAD_DIM, SSD_STATE))
        x = _ffn_call(x.reshape(bp * seq, D_MODEL), ffn_p[layer]).reshape(bp, seq, D_MODEL)
    y_prompt = x

    bs, sl, _ = x_sample.shape
    n = bs * sl
    xs = x_sample.reshape(n, D_MODEL)
    s_lc, s_lh, s_sc, s_ss = [], [], [], []
    for layer in range(DEPTH):
        j = layer // 2
        fp = ffn_p[layer]
        if layer % 2 == 0:
            lp = lru_p[j]
            g, rec = _dense_call(_lru_pre_kernel, "lru_pre", (xs, lp['gpre'], lp['win']),
                                 (D_RNN, D_RNN), [])
            gated, conv, h = _lru_core_sample_call(
                g.reshape(bs, sl, D_RNN), rec.reshape(bs, sl, D_RNN), state_lru_conv[j],
                state_lru_h[j].reshape(bs, 1, D_RNN), lp)
            s_lc.append(conv[:, CARRY_LO:])
            s_lh.append(h[:, 0])
            (xs,) = _dense_call(
                _lru_post_ffn_kernel, "lru_post_ffn",
                (xs, gated.reshape(n, D_RNN), lp['wout'], lp['gpost'], fp['gpre'], fp['gpost'],
                 fp['wg'], fp['wu'], fp['wd']),
                (D_MODEL,), [pltpu.VMEM((n, D_FF), BF16)])
        else:
            sp = ssd_p[j]
            z, xbc, dt = _dense_call(_ssd_pre_kernel, "ssd_pre",
                                     (xs, sp['gpre'], sp['wzx'], sp['wdt'], sp['dtb']),
                                     (D_INNER, SSD_CONV_DIM, LANES), [])
            y, conv, st = _ssd_core_sample_call(
                xbc.reshape(bs, sl, SSD_CONV_DIM), dt.reshape(bs, sl, LANES), state_ssd_conv[j],
                state_ssd[j].reshape(bs, D_INNER, SSD_STATE), sp)
            s_sc.append(conv[:, CARRY_LO:])
            s_ss.append(st.reshape(bs, SSD_HEADS, SSD_HEAD_DIM, SSD_STATE))
            (xs,) = _dense_call(
                _ssd_post_ffn_kernel, "ssd_post_ffn",
                (xs, y.reshape(n, D_INNER), z, sp['normg'], sp['wout'], sp['gpost'],
                 fp['gpre'], fp['gpost'], fp['wg'], fp['wu'], fp['wd']),
                (D_MODEL,), [pltpu.VMEM((n, D_INNER), BF16), pltpu.VMEM((n, D_FF), BF16)])
    y_sample = xs.reshape(bs, sl, D_MODEL)

    return (y_prompt, y_sample,
            jnp.stack(p_lc), jnp.stack(p_lh), jnp.stack(p_sc), jnp.stack(p_ss),
            jnp.stack(s_lc), jnp.stack(s_lh), jnp.stack(s_sc), jnp.stack(s_ss))
```

```python
import functools
import math

import jax
import jax.numpy as jnp
from jax import lax
from jax.experimental import pallas as pl
from jax.experimental.pallas import tpu as pltpu

F32 = jnp.float32
BF16 = jnp.bfloat16

D_MODEL = 1024
DEPTH = 4
CONV_WIDTH = 4
EPS = 1e-6
D_RNN = D_MODEL
LRU_BLOCKS = 4
LRU_BLOCK_W = D_RNN // LRU_BLOCKS
LRU_C = 8.0
D_INNER = 2 * D_MODEL
SSD_HEAD_DIM = 64
SSD_HEADS = D_INNER // SSD_HEAD_DIM
SSD_GROUPS = 4
SSD_HPG = SSD_HEADS // SSD_GROUPS
SSD_STATE = 128
SSD_GN = SSD_GROUPS * SSD_STATE
SSD_CONV_DIM = D_INNER + 2 * SSD_GN
D_FF = -(-8 * D_MODEL // (3 * 256)) * 256

LANES = 128
SUBLANES = 8
CARRY_ROWS = SUBLANES
CARRY_LO = CARRY_ROWS - (CONV_WIDTH - 1)
SSD_Q = 128
GROUP_W = D_INNER // SSD_GROUPS
PAIR_W = 2 * SSD_HEAD_DIM
FF_BLK = 256
PROJ_BLK = 512
PROMPT_T = 256
VMEM_LIMIT = 60000 * 1024


def _rmsnorm(x, g):
    ms = jnp.mean(x * x, axis=-1, keepdims=True)
    return x * lax.rsqrt(ms + EPS) * g


def _softplus(x):
    return jnp.maximum(x, 0.0) + jnp.log1p(jnp.exp(-jnp.abs(x)))


def _silu(x):
    return x * jax.nn.sigmoid(x)


def _gelu_tanh(x):
    c = math.sqrt(2.0 / math.pi)
    return x * (0.5 * (1.0 + jnp.tanh(c * (x + 0.044715 * (x * x * x)))))


def _dot(a, b):
    return jnp.dot(a, b, preferred_element_type=F32)


def _conv_taps(full_ref, rows, w, b, col):
    out = b + full_ref[CARRY_LO:CARRY_LO + rows, col] * w[0:1]
    for k in range(1, CONV_WIDTH):
        out = out + full_ref[CARRY_LO + k:CARRY_LO + k + rows, col] * w[k:k + 1]
    return out


def _run(steps):
    try:
        while True:
            next(steps)
    except StopIteration as done:
        return done.value


def _weave(main, side):
    side_result = None
    side_done = False
    try:
        while True:
            want = next(main)
            for _ in range(want):
                if not side_done:
                    try:
                        next(side)
                    except StopIteration as done:
                        side_result, side_done = done.value, True
    except StopIteration as done:
        main_result = done.value
    if not side_done:
        side_result = _run(side)
    return main_result, side_result


def _ffn_steps(load_x, gpre, gpost, wg_ref, wu_ref, wd_ref, act_ref):
    hn = _rmsnorm(load_x(), gpre).astype(BF16)
    yield
    for c in range(D_FF // FF_BLK):
        col = slice(c * FF_BLK, (c + 1) * FF_BLK)
        gate = _dot(hn, wg_ref[:, col])
        up = _dot(hn, wu_ref[:, col])
        act_ref[:, col] = (_silu(gate) * up).astype(BF16)
        yield
    parts = []
    for c in range(D_MODEL // FF_BLK):
        parts.append(_dot(act_ref[...], wd_ref[:, c * FF_BLK:(c + 1) * FF_BLK]))
        yield
    f = jnp.concatenate(parts, axis=1)
    return load_x() + _rmsnorm(f, gpost)


def _ffn(x, gpre, gpost, wg_ref, wu_ref, wd_ref, act_ref):
    return _run(_ffn_steps(lambda: x, gpre, gpost, wg_ref, wu_ref, wd_ref, act_ref))


def _lru_pre(x, gpre, win_ref):
    hn = _rmsnorm(x, gpre).astype(BF16)
    gate = _dot(hn, win_ref[:, :D_RNN])
    rec = _dot(hn, win_ref[:, D_RNN:])
    return _gelu_tanh(gate), rec


def _scan8(a, u):
    rows, ch = a.shape
    a3 = a.reshape(rows // SUBLANES, SUBLANES, ch)
    u3 = u.reshape(rows // SUBLANES, SUBLANES, ch)
    sub = lax.broadcasted_iota(jnp.int32, a3.shape, 1)
    for s in (1, 2, 4):
        m = sub >= s
        a_sh = jnp.where(m, pltpu.roll(a3, s, 1), 1.0)
        u_sh = jnp.where(m, pltpu.roll(u3, s, 1), 0.0)
        u3 = a3 * u_sh + u3
        a3 = a3 * a_sh
    return a3.reshape(rows, ch), u3.reshape(rows, ch)


def _lru_core_steps(rows, full_ref, g_ref, a_ref, u_ref, h_b, cw, cb, wr_ref, br, wi_ref, bi, lam):
    sp = _softplus(-lam)
    for k in range(LRU_BLOCKS):
        yield
        col = slice(k * LRU_BLOCK_W, (k + 1) * LRU_BLOCK_W)
        xc = _conv_taps(full_ref, rows, cw[:, col], cb[:, col], col)
        xcb = xc.astype(BF16)
        r = jax.nn.sigmoid(_dot(xcb, wr_ref[k]) + br[:, col])
        i = jax.nn.sigmoid(_dot(xcb, wi_ref[k]) + bi[:, col])
        log_a = (-LRU_C * r) * sp[:, col]
        a = jnp.exp(log_a)
        u = jnp.sqrt(-jnp.tanh(log_a) * (a * a + 1.0)) * (i * xc)
        a8, u8 = _scan8(a, u)
        a_ref[:, col] = a8
        u_ref[:, col] = u8

    yield
    for j in range(rows // SUBLANES):
        blk = slice(j * SUBLANES, (j + 1) * SUBLANES)
        hs = u_ref[blk, :] + a_ref[blk, :] * h_b
        u_ref[blk, :] = hs
        h_b = jnp.broadcast_to(hs[SUBLANES - 1:SUBLANES, :], hs.shape)
    return u_ref[...] * g_ref[...], h_b


def _lru_core(*args):
    return _run(_lru_core_steps(*args))


def _lru_post(x, gated, wout_ref, gpost):
    mix = _dot(gated.astype(BF16), wout_ref[...])
    return x + _rmsnorm(mix, gpost)


def _lru_layer_prompt_kernel(x_ref, gpre_ref, gpost_ref, win_ref, cw_ref, cb_ref, wr_ref, br_ref,
                             wi_ref, bi_ref, lam_ref, wout_ref,
                             fpre_ref, fpost_ref, wg_ref, wu_ref, wd_ref,
                             o_ref, conv_o_ref, h_o_ref,
                             full_ref, g_ref, a_ref, u_ref, h_ref, xmid_ref, act_ref):
    rows = x_ref.shape[0]
    step = pl.program_id(1)

    @pl.when(step == 0)
    def _():
        full_ref[0:CARRY_ROWS, :] = jnp.zeros((CARRY_ROWS, D_RNN), F32)
        h_ref[...] = jnp.zeros(h_ref.shape, F32)
        xmid_ref[...] = jnp.zeros(xmid_ref.shape, F32)

    def mixer():
        x = x_ref[...]
        hn = _rmsnorm(x, gpre_ref[...]).astype(BF16)
        g_ref[...] = _gelu_tanh(_dot(hn, win_ref[:, :D_RNN]))
        yield 2
        full_ref[CARRY_ROWS:CARRY_ROWS + rows, :] = _dot(hn, win_ref[:, D_RNN:])
        core = _lru_core_steps(rows, full_ref, g_ref, a_ref, u_ref, h_ref[...], cw_ref[...], cb_ref[...],
                               wr_ref, br_ref[...], wi_ref, bi_ref[...], lam_ref[...])
        pieces = 0
        try:
            while True:
                next(core)
                pieces += 1
                yield 3 if pieces <= LRU_BLOCKS else 1
        except StopIteration as done:
            gated, h_b = done.value
        h_ref[...] = h_b
        tail = full_ref[rows:rows + CARRY_ROWS, :]
        full_ref[0:CARRY_ROWS, :] = tail
        yield 1
        return _lru_post(x, gated, wout_ref, gpost_ref[...]), tail, h_b

    ffn = _ffn_steps(lambda: xmid_ref[...], fpre_ref[...], fpost_ref[...], wg_ref, wu_ref, wd_ref, act_ref)
    (x_new, tail, h_b), o_ref[...] = _weave(mixer(), ffn)
    xmid_ref[...] = x_new

    @pl.when(step == pl.num_programs(1) - 2)
    def _():
        conv_o_ref[...] = tail
        h_o_ref[...] = h_b[0:1, :]


def _lru_pre_kernel(x_ref, gpre_ref, win_ref, g_o_ref, rec_o_ref):
    g, rec = _lru_pre(x_ref[...], gpre_ref[...], win_ref)
    g_o_ref[...] = g
    rec_o_ref[...] = rec


def _lru_core_sample_kernel(g_ref, rec_ref, conv_ref, h_ref, cw_ref, cb_ref, wr_ref, br_ref,
                            wi_ref, bi_ref, lam_ref,
                            gated_o_ref, conv_o_ref, h_o_ref,
                            full_ref, a_ref, u_ref):
    rows = rec_ref.shape[0]
    full_ref[0:CARRY_ROWS, :] = jnp.zeros((CARRY_ROWS, D_RNN), F32)
    full_ref[CARRY_LO:CARRY_ROWS, :] = conv_ref[...]
    full_ref[CARRY_ROWS:CARRY_ROWS + rows, :] = rec_ref[...]
    h_b = jnp.broadcast_to(h_ref[...], (SUBLANES, D_RNN))
    gated, h_b = _lru_core(rows, full_ref, g_ref, a_ref, u_ref, h_b, cw_ref[...], cb_ref[...],
                           wr_ref, br_ref[...], wi_ref, bi_ref[...], lam_ref[...])
    gated_o_ref[...] = gated
    conv_o_ref[...] = full_ref[rows:rows + CARRY_ROWS, :]
    h_o_ref[...] = h_b[0:1, :]


def _lru_post_ffn_kernel(x_ref, gated_ref, wout_ref, gpost_ref, fpre_ref, fpost_ref,
                         wg_ref, wu_ref, wd_ref, o_ref, act_ref):
    x = _lru_post(x_ref[...], gated_ref[...], wout_ref, gpost_ref[...])
    o_ref[...] = _ffn(x, fpre_ref[...], fpost_ref[...], wg_ref, wu_ref, wd_ref, act_ref)


def _ssd_pre(x, gpre, wzx_ref, wdt_ref, dtb, z_ref, xbc_ref, xbc_row0):
    rows = x.shape[0]
    hn = _rmsnorm(x, gpre).astype(BF16)
    for c in range(D_INNER // PROJ_BLK):
        col = slice(c * PROJ_BLK, (c + 1) * PROJ_BLK)
        z_ref[:, col] = _dot(hn, wzx_ref[:, col])
    for c in range(SSD_CONV_DIM // PROJ_BLK):
        col = slice(c * PROJ_BLK, (c + 1) * PROJ_BLK)
        wcol = slice(D_INNER + c * PROJ_BLK, D_INNER + (c + 1) * PROJ_BLK)
        xbc_ref[xbc_row0:xbc_row0 + rows, col] = _dot(hn, wzx_ref[:, wcol])
    return _softplus(_dot(hn, wdt_ref[...]) + dtb)


def _ssd_conv_silu(full_ref, rows, cw, cb):
    for c in range(SSD_CONV_DIM // PROJ_BLK):
        col = slice(c * PROJ_BLK, (c + 1) * PROJ_BLK)
        out = _conv_taps(full_ref, rows, cw[:, col], cb[:, col], col)
        full_ref[CARRY_ROWS:CARRY_ROWS + rows, col] = _silu(out)


def _cumsum_rows(x):
    rows = x.shape[0]
    row = lax.broadcasted_iota(jnp.int32, x.shape, 0)
    s = 1
    while s < rows:
        x = x + jnp.where(row >= s, pltpu.roll(x, s, 0), 0.0)
        s *= 2
    return x


def _ssd_core_steps(rows, full_ref, dt_ref, a_neg, d_exp, st_ref, y_ref):
    q = SSD_Q
    tril = (lax.broadcasted_iota(jnp.int32, (q, q), 0) >= lax.broadcasted_iota(jnp.int32, (q, q), 1))
    left = lax.broadcasted_iota(jnp.int32, (q, PAIR_W), 1) < SSD_HEAD_DIM
    contract_last = (((1,), (1,)), ((), ()))
    contract_first = (((0,), (0,)), ((), ()))

    for c in range(rows // q):
        r0 = c * q
        rf = c * q + CARRY_ROWS
        dtc = dt_ref[pl.ds(r0, q), :]
        acs = _cumsum_rows(dtc * a_neg)
        acs_t = acs.T
        for g in range(SSD_GROUPS):
            yield
            gcol = slice(g * GROUP_W, (g + 1) * GROUP_W)
            bcol = slice(D_INNER + g * SSD_STATE, D_INNER + (g + 1) * SSD_STATE)
            ccol = slice(D_INNER + SSD_GN + g * SSD_STATE, D_INNER + SSD_GN + (g + 1) * SSD_STATE)
            b_g = full_ref[pl.ds(rf, q), bcol].astype(BF16)
            c_g = full_ref[pl.ds(rf, q), ccol].astype(BF16)
            cb = lax.dot_general(c_g, b_g, contract_last, preferred_element_type=F32)
            y_off = _dot(c_g, st_ref[:, gcol].astype(BF16))
            xdd = []
            dec = []
            for pr in range(SSD_HPG // 2):
                h1 = g * SSD_HPG + 2 * pr
                h2 = h1 + 1
                pcol = slice(h1 * SSD_HEAD_DIM, h1 * SSD_HEAD_DIM + PAIR_W)
                a1 = jnp.broadcast_to(acs[:, h1:h1 + 1], (q, q))
                a2 = jnp.broadcast_to(acs[:, h2:h2 + 1], (q, q))
                l1 = jnp.exp(jnp.where(tril, a1 - acs_t[h1:h1 + 1, :], -jnp.inf))
                l2 = jnp.exp(jnp.where(tril, a2 - acs_t[h2:h2 + 1, :], -jnp.inf))
                m = jnp.concatenate([cb * l1, cb * l2], axis=1).astype(BF16)
                a_pair = jnp.where(left, a1, a2)
                dt_pair = jnp.where(left, jnp.broadcast_to(dtc[:, h1:h1 + 1], (q, PAIR_W)),
                                    jnp.broadcast_to(dtc[:, h2:h2 + 1], (q, PAIR_W)))
                xs = full_ref[pl.ds(rf, q), pcol]
                xdt = xs * dt_pair
                rhs = jnp.concatenate([jnp.where(left, xdt, 0.0), jnp.where(left, 0.0, xdt)],
                                      axis=0).astype(BF16)
                y_diag = _dot(m, rhs)
                a_last = a_pair[q - 1:q, :]
                xdd.append((xdt * jnp.exp(a_last - a_pair)).astype(BF16))
                dec.append(jnp.exp(a_last))
                ocol = slice(pr * PAIR_W, (pr + 1) * PAIR_W)
                y_ref[pl.ds(r0, q), pcol] = (y_diag + y_off[:, ocol] * jnp.exp(a_pair)
                                             + d_exp[:, pcol] * xs)
            states = lax.dot_general(b_g, jnp.concatenate(xdd, axis=1), contract_first,
                                     preferred_element_type=F32)
            st_ref[:, gcol] = st_ref[:, gcol] * jnp.concatenate(dec, axis=1) + states


def _ssd_core(*args):
    return _run(_ssd_core_steps(*args))


def _ssd_post_steps(x, y_ref, z_ref, normg, wout_ref, gpost, ybf_ref):
    for g in range(SSD_GROUPS):
        yield
        col = slice(g * GROUP_W, (g + 1) * GROUP_W)
        yg = y_ref[:, col] * _silu(z_ref[:, col])
        ms = jnp.mean(yg * yg, axis=-1, keepdims=True)
        ybf_ref[:, col] = (yg * lax.rsqrt(ms + EPS) * normg[:, col]).astype(BF16)
    mix = _dot(ybf_ref[...], wout_ref[...])
    return x + _rmsnorm(mix, gpost)


def _ssd_post(*args):
    return _run(_ssd_post_steps(*args))


def _ssd_layer_prompt_kernel(x_ref, gpre_ref, gpost_ref, wzx_ref, wdt_ref, dtb_ref, cw_ref, cb_ref,
                             alog_ref, dexp_ref, normg_ref, wout_ref,
                             fpre_ref, fpost_ref, wg_ref, wu_ref, wd_ref,
                             o_ref, conv_o_ref, st_o_ref,
                             full_ref, z_ref, dt_ref, y_ref, ybf_ref, st_ref, xmid_ref, act_ref):
    rows = x_ref.shape[0]
    step = pl.program_id(1)

    @pl.when(step == 0)
    def _():
        full_ref[0:CARRY_ROWS, :] = jnp.zeros((CARRY_ROWS, SSD_CONV_DIM), F32)
        st_ref[...] = jnp.zeros(st_ref.shape, F32)
        xmid_ref[...] = jnp.zeros(xmid_ref.shape, F32)

    def mixer():
        x = x_ref[...]
        hn = _rmsnorm(x, gpre_ref[...]).astype(BF16)
        dt_ref[...] = _softplus(_dot(hn, wdt_ref[...]) + dtb_ref[...])
        cw = cw_ref[...]
        cb = cb_ref[...]
        n_blk = SSD_CONV_DIM // PROJ_BLK

        def project(c):
            wcol = slice(D_INNER + c * PROJ_BLK, D_INNER + (c + 1) * PROJ_BLK)
            full_ref[CARRY_ROWS:CARRY_ROWS + rows, c * PROJ_BLK:(c + 1) * PROJ_BLK] = _dot(hn, wzx_ref[:, wcol])

        project(0)
        for c in range(n_blk):
            if c + 1 < n_blk:
                project(c + 1)
            col = slice(c * PROJ_BLK, (c + 1) * PROJ_BLK)
            tail_c = full_ref[rows:rows + CARRY_ROWS, col]
            out = _conv_taps(full_ref, rows, cw[:, col], cb[:, col], col)
            full_ref[CARRY_ROWS:CARRY_ROWS + rows, col] = _silu(out)
            full_ref[0:CARRY_ROWS, col] = tail_c
            yield 1

        z_blocks = iter(range(D_INNER // PROJ_BLK))
        core = _ssd_core_steps(rows, full_ref, dt_ref, -jnp.exp(alog_ref[...]), dexp_ref[...], st_ref, y_ref)
        for _ in core:
            c = next(z_blocks, None)
            if c is None:
                yield 1
            else:
                col = slice(c * PROJ_BLK, (c + 1) * PROJ_BLK)
                z_ref[:, col] = _dot(hn, wzx_ref[:, col])
                yield c % 2
        post = _ssd_post_steps(x, y_ref, z_ref, normg_ref[...], wout_ref, gpost_ref[...], ybf_ref)
        try:
            while True:
                next(post)
                yield 1
        except StopIteration as done:
            return done.value

    ffn = _ffn_steps(lambda: xmid_ref[...], fpre_ref[...], fpost_ref[...], wg_ref, wu_ref, wd_ref, act_ref)
    x_new, o_ref[...] = _weave(mixer(), ffn)
    xmid_ref[...] = x_new

    @pl.when(step == pl.num_programs(1) - 2)
    def _():
        conv_o_ref[...] = full_ref[0:CARRY_ROWS, :]
        st_o_ref[...] = st_ref[...].T


def _ssd_pre_kernel(x_ref, gpre_ref, wzx_ref, wdt_ref, dtb_ref, z_o_ref, xbc_o_ref, dt_o_ref):
    dt_o_ref[...] = _ssd_pre(x_ref[...], gpre_ref[...], wzx_ref, wdt_ref, dtb_ref[...],
                             z_o_ref, xbc_o_ref, 0)


def _ssd_core_sample_kernel(xbc_ref, dt_in_ref, conv_ref, st_in_ref, cw_ref, cb_ref, alog_ref, dexp_ref,
                            y_o_ref, conv_o_ref, st_o_ref,
                            full_ref, dt_ref, y_ref, st_ref):
    rows = xbc_ref.shape[0]
    full_ref[...] = jnp.zeros(full_ref.shape, F32)
    full_ref[CARRY_LO:CARRY_ROWS, :] = conv_ref[...]
    full_ref[CARRY_ROWS:CARRY_ROWS + rows, :] = xbc_ref[...]
    dt_ref[...] = jnp.zeros(dt_ref.shape, F32)
    dt_ref[0:rows, :] = dt_in_ref[...]
    conv_o_ref[...] = full_ref[rows:rows + CARRY_ROWS, :]
    _ssd_conv_silu(full_ref, rows, cw_ref[...], cb_ref[...])
    st_ref[...] = st_in_ref[...].T
    _ssd_core(SSD_Q, full_ref, dt_ref, -jnp.exp(alog_ref[...]), dexp_ref[...], st_ref, y_ref)
    y_o_ref[...] = y_ref[0:rows, :]
    st_o_ref[...] = st_ref[...].T


def _ssd_core_sample_kernel_aliased(xbc_ref, dt_in_ref, conv_ref, st_in_ref, cw_ref, cb_ref, alog_ref,
                                    dexp_ref, st_prev_layers_ref, *rest):
    del st_prev_layers_ref
    _ssd_core_sample_kernel(xbc_ref, dt_in_ref, conv_ref, st_in_ref, cw_ref, cb_ref, alog_ref, dexp_ref, *rest)


def _ssd_post_ffn_kernel(x_ref, y_ref, z_ref, normg_ref, wout_ref, gpost_ref, fpre_ref, fpost_ref,
                         wg_ref, wu_ref, wd_ref, o_ref, ybf_ref, act_ref):
    x = _ssd_post(x_ref[...], y_ref, z_ref, normg_ref[...], wout_ref, gpost_ref[...], ybf_ref)
    o_ref[...] = _ffn(x, fpre_ref[...], fpost_ref[...], wg_ref, wu_ref, wd_ref, act_ref)


def _whole(shape):
    nd = len(shape)
    return pl.BlockSpec(shape, lambda *_: (0,) * nd, pipeline_mode=pl.Buffered(1))


def _params(n_axes):
    return pltpu.CompilerParams(dimension_semantics=("arbitrary",) * n_axes,
                                vmem_limit_bytes=VMEM_LIMIT)


def _prompt_layer_call(kernel, name, x, consts, state_shapes, scratch):
    bsz, seq, _ = x.shape
    t = PROMPT_T
    nt = seq // t
    return pl.pallas_call(
        kernel,
        grid=(bsz, nt + 1),
        in_specs=[pl.BlockSpec((None, t, D_MODEL), lambda b, i: (b, jnp.minimum(i, nt - 1), 0))]
        + [_whole(c.shape) for c in consts],
        out_specs=[pl.BlockSpec((None, t, D_MODEL), lambda b, i: (b, jnp.maximum(i - 1, 0), 0))]
        + [pl.BlockSpec((None,) + s, lambda b, i: (b, 0, 0)) for s in state_shapes],
        out_shape=[jax.ShapeDtypeStruct((bsz, seq, D_MODEL), F32)]
        + [jax.ShapeDtypeStruct((bsz,) + s, F32) for s in state_shapes],
        scratch_shapes=scratch + [pltpu.VMEM((t, D_MODEL), F32), pltpu.VMEM((t, D_FF), BF16)],
        compiler_params=_params(2),
        name=name,
    )(x, *consts)


def _ffn_consts(fp):
    return (fp['gpre'], fp['gpost'], fp['wg'], fp['wu'], fp['wd'])


def _lru_prompt_call(x, lp, fp):
    t = PROMPT_T
    consts = (lp['gpre'], lp['gpost'], lp['win'], lp['cw'], lp['cb'], lp['wr'], lp['br'],
              lp['wi'], lp['bi'], lp['lam'], lp['wout']) + _ffn_consts(fp)
    return _prompt_layer_call(
        _lru_layer_prompt_kernel, "lru_layer_prompt", x, consts,
        [(CARRY_ROWS, D_RNN), (1, D_RNN)],
        [pltpu.VMEM((t + CARRY_ROWS, D_RNN), F32),
         pltpu.VMEM((t, D_RNN), F32),
         pltpu.VMEM((t, D_RNN), F32),
         pltpu.VMEM((t, D_RNN), F32),
         pltpu.VMEM((SUBLANES, D_RNN), F32)])


def _ssd_prompt_call(x, sp, fp):
    t = PROMPT_T
    consts = (sp['gpre'], sp['gpost'], sp['wzx'], sp['wdt'], sp['dtb'], sp['cw'], sp['cb'],
              sp['alog'], sp['dexp'], sp['normg'], sp['wout']) + _ffn_consts(fp)
    return _prompt_layer_call(
        _ssd_layer_prompt_kernel, "ssd_layer_prompt", x, consts,
        [(CARRY_ROWS, SSD_CONV_DIM), (D_INNER, SSD_STATE)],
        [pltpu.VMEM((t + CARRY_ROWS, SSD_CONV_DIM), F32),
         pltpu.VMEM((t, D_INNER), F32),
         pltpu.VMEM((t, LANES), F32),
         pltpu.VMEM((t, D_INNER), F32),
         pltpu.VMEM((t, D_INNER), BF16),
         pltpu.VMEM((SSD_STATE, D_INNER), F32)])


def _dense_call(kernel, name, ins, out_widths, scratch):
    n = ins[0].shape[0]
    return pl.pallas_call(
        kernel,
        out_shape=[jax.ShapeDtypeStruct((n, w), F32) for w in out_widths],
        scratch_shapes=scratch,
        compiler_params=pltpu.CompilerParams(vmem_limit_bytes=VMEM_LIMIT),
        name=name,
    )(*ins)


def _lru_core_sample_call(g, rec, conv, h, lp):
    bsz, seq, _ = g.shape
    consts = (lp['cw'], lp['cb'], lp['wr'], lp['br'], lp['wi'], lp['bi'], lp['lam'])

    def per_seq(rows, width):
        return pl.BlockSpec((None, rows, width), lambda b: (b, 0, 0))

    return pl.pallas_call(
        _lru_core_sample_kernel,
        grid=(bsz,),
        in_specs=[per_seq(seq, D_RNN), per_seq(seq, D_RNN), per_seq(CONV_WIDTH - 1, D_RNN),
                  per_seq(1, D_RNN)] + [_whole(c.shape) for c in consts],
        out_specs=[per_seq(seq, D_RNN), per_seq(CARRY_ROWS, D_RNN), per_seq(1, D_RNN)],
        out_shape=[jax.ShapeDtypeStruct((bsz, seq, D_RNN), F32),
                   jax.ShapeDtypeStruct((bsz, CARRY_ROWS, D_RNN), F32),
                   jax.ShapeDtypeStruct((bsz, 1, D_RNN), F32)],
        scratch_shapes=[pltpu.VMEM((seq + CARRY_ROWS, D_RNN), F32),
                        pltpu.VMEM((seq, D_RNN), F32),
                        pltpu.VMEM((seq, D_RNN), F32)],
        compiler_params=_params(1),
        name="lru_core_sample",
    )(g, rec, conv, h, *consts)


def _ssd_core_sample_call(xbc, dt, conv, st_all, j, st_new_all, sp):
    bsz, seq, _ = xbc.shape
    consts = (sp['cw'], sp['cb'], sp['alog'], sp['dexp'])

    def per_seq(rows, width):
        return pl.BlockSpec((None, rows, width), lambda b: (b, 0, 0))

    layer_spec = pl.BlockSpec((None, None, D_INNER, SSD_STATE), lambda b: (j, b, 0, 0))
    in_specs = [per_seq(seq, SSD_CONV_DIM), per_seq(seq, LANES), per_seq(CONV_WIDTH - 1, SSD_CONV_DIM),
                layer_spec] + [_whole(c.shape) for c in consts]
    args = [xbc, dt, conv, st_all, *consts]
    kernel = _ssd_core_sample_kernel
    aliases = {}
    if st_new_all is not None:
        in_specs.append(pl.BlockSpec(memory_space=pl.ANY))
        aliases = {len(args): 2}
        args.append(st_new_all)
        kernel = _ssd_core_sample_kernel_aliased
    return pl.pallas_call(
        kernel,
        grid=(bsz,),
        in_specs=in_specs,
        out_specs=[per_seq(seq, D_INNER), per_seq(CARRY_ROWS, SSD_CONV_DIM), layer_spec],
        out_shape=[jax.ShapeDtypeStruct((bsz, seq, D_INNER), F32),
                   jax.ShapeDtypeStruct((bsz, CARRY_ROWS, SSD_CONV_DIM), F32),
                   jax.ShapeDtypeStruct(st_all.shape, F32)],
        scratch_shapes=[pltpu.VMEM((SSD_Q + CARRY_ROWS, SSD_CONV_DIM), F32),
                        pltpu.VMEM((SSD_Q, LANES), F32),
                        pltpu.VMEM((SSD_Q, D_INNER), F32),
                        pltpu.VMEM((SSD_STATE, D_INNER), F32)],
        input_output_aliases=aliases,
        compiler_params=_params(1),
        name="ssd_core_sample",
    )(*args)


def _row(v):
    return v.reshape(1, -1).astype(F32)


def _pad_lanes(v2d):
    return jnp.pad(v2d, ((0, 0), (0, LANES - v2d.shape[-1])))


def kernel(x_prompt, x_sample, state_lru_conv, state_lru_h, state_ssd_conv, state_ssd, norm_mix_pre, norm_mix_post, norm_ffn_pre, norm_ffn_post, lru_w_in, lru_conv_w, lru_conv_b, lru_w_r, lru_b_r, lru_w_i, lru_b_i, lru_lambda, lru_w_out, ssd_w_in, ssd_conv_w, ssd_conv_b, ssd_dt_bias, ssd_a_log, ssd_d, ssd_norm, ssd_w_out, ffn_w_gate, ffn_w_up, ffn_w_down):
    n_a = lru_w_in.shape[0]
    n_b = ssd_w_in.shape[0]
    lru_p = []
    for j in range(n_a):
        layer = 2 * j
        lru_p.append(dict(
            gpre=_row(norm_mix_pre[layer]), gpost=_row(norm_mix_post[layer]),
            win=lru_w_in[j].astype(BF16), cw=lru_conv_w[j], cb=_row(lru_conv_b[j]),
            wr=lru_w_r[j].astype(BF16), br=_row(lru_b_r[j]),
            wi=lru_w_i[j].astype(BF16), bi=_row(lru_b_i[j]),
            lam=_row(lru_lambda[j]), wout=lru_w_out[j].astype(BF16)))
    ssd_p = []
    for j in range(n_b):
        layer = 2 * j + 1
        w_in = ssd_w_in[j]
        ssd_p.append(dict(
            gpre=_row(norm_mix_pre[layer]), gpost=_row(norm_mix_post[layer]),
            wzx=w_in[:, :D_INNER + SSD_CONV_DIM].astype(BF16),
            wdt=_pad_lanes(w_in[:, D_INNER + SSD_CONV_DIM:]).astype(BF16),
            dtb=_pad_lanes(_row(ssd_dt_bias[j])),
            cw=ssd_conv_w[j], cb=_row(ssd_conv_b[j]),
            alog=_pad_lanes(_row(ssd_a_log[j])),
            dexp=_row(jnp.repeat(ssd_d[j], SSD_HEAD_DIM)),
            normg=_row(ssd_norm[j]), wout=ssd_w_out[j].astype(BF16)))
    ffn_p = []
    for layer in range(DEPTH):
        ffn_p.append(dict(
            gpre=_row(norm_ffn_pre[layer]), gpost=_row(norm_ffn_post[layer]),
            wg=ffn_w_gate[layer].astype(BF16), wu=ffn_w_up[layer].astype(BF16),
            wd=ffn_w_down[layer].astype(BF16)))

    bp, seq, _ = x_prompt.shape
    x = x_prompt
    p_lc, p_lh, p_sc, p_ss = [], [], [], []
    for layer in range(DEPTH):
        j = layer // 2
        if layer % 2 == 0:
            x, conv, h = _lru_prompt_call(x, lru_p[j], ffn_p[layer])
            p_lc.append(conv[:, CARRY_LO:])
            p_lh.append(h[:, 0])
        else:
            x, conv, st = _ssd_prompt_call(x, ssd_p[j], ffn_p[layer])
            p_sc.append(conv[:, CARRY_LO:])
            p_ss.append(st.reshape(bp, SSD_HEADS, SSD_HEAD_DIM, SSD_STATE))
    y_prompt = x

    bs, sl, _ = x_sample.shape
    n = bs * sl
    xs = x_sample.reshape(n, D_MODEL)
    s_lc, s_lh, s_sc = [], [], []
    s_ss_all = None
    for layer in range(DEPTH):
        j = layer // 2
        fp = ffn_p[layer]
        if layer % 2 == 0:
            lp = lru_p[j]
            g, rec = _dense_call(_lru_pre_kernel, "lru_pre", (xs, lp['gpre'], lp['win']),
                                 (D_RNN, D_RNN), [])
            gated, conv, h = _lru_core_sample_call(
                g.reshape(bs, sl, D_RNN), rec.reshape(bs, sl, D_RNN), state_lru_conv[j],
                state_lru_h[j].reshape(bs, 1, D_RNN), lp)
            s_lc.append(conv[:, CARRY_LO:])
            s_lh.append(h[:, 0])
            (xs,) = _dense_call(
                _lru_post_ffn_kernel, "lru_post_ffn",
                (xs, gated.reshape(n, D_RNN), lp['wout'], lp['gpost'], fp['gpre'], fp['gpost'],
                 fp['wg'], fp['wu'], fp['wd']),
                (D_MODEL,), [pltpu.VMEM((n, D_FF), BF16)])
        else:
            sp = ssd_p[j]
            z, xbc, dt = _dense_call(_ssd_pre_kernel, "ssd_pre",
                                     (xs, sp['gpre'], sp['wzx'], sp['wdt'], sp['dtb']),
                                     (D_INNER, SSD_CONV_DIM, LANES), [])
            y, conv, s_ss_all = _ssd_core_sample_call(
                xbc.reshape(bs, sl, SSD_CONV_DIM), dt.reshape(bs, sl, LANES), state_ssd_conv[j],
                state_ssd.reshape(n_b, bs, D_INNER, SSD_STATE), j, s_ss_all, sp)
            s_sc.append(conv[:, CARRY_LO:])
            (xs,) = _dense_call(
                _ssd_post_ffn_kernel, "ssd_post_ffn",
                (xs, y.reshape(n, D_INNER), z, sp['normg'], sp['wout'], sp['gpost'],
                 fp['gpre'], fp['gpost'], fp['wg'], fp['wu'], fp['wd']),
                (D_MODEL,), [pltpu.VMEM((n, D_INNER), BF16), pltpu.VMEM((n, D_FF), BF16)])
    y_sample = xs.reshape(bs, sl, D_MODEL)

    return (y_prompt, y_sample,
            jnp.stack(p_lc), jnp.stack(p_lh), jnp.stack(p_sc), jnp.stack(p_ss),
            jnp.stack(s_lc), jnp.stack(s_lh), jnp.stack(s_sc),
            s_ss_all.reshape(n_b, bs, SSD_HEADS, SSD_HEAD_DIM, SSD_STATE))
```

```python
import functools
import math

import jax
import jax.numpy as jnp
from jax import lax
from jax.experimental import pallas as pl
from jax.experimental.pallas import tpu as pltpu

F32 = jnp.float32
BF16 = jnp.bfloat16

D_MODEL = 1024
DEPTH = 4
CONV_WIDTH = 4
EPS = 1e-6
D_RNN = D_MODEL
LRU_BLOCKS = 4
LRU_BLOCK_W = D_RNN // LRU_BLOCKS
LRU_C = 8.0
D_INNER = 2 * D_MODEL
SSD_HEAD_DIM = 64
SSD_HEADS = D_INNER // SSD_HEAD_DIM
SSD_GROUPS = 4
SSD_HPG = SSD_HEADS // SSD_GROUPS
SSD_STATE = 128
SSD_GN = SSD_GROUPS * SSD_STATE
SSD_CONV_DIM = D_INNER + 2 * SSD_GN
D_FF = -(-8 * D_MODEL // (3 * 256)) * 256

LANES = 128
SUBLANES = 8
CARRY_ROWS = SUBLANES
CARRY_LO = CARRY_ROWS - (CONV_WIDTH - 1)
SSD_Q = 128
GROUP_W = D_INNER // SSD_GROUPS
PAIR_W = 2 * SSD_HEAD_DIM
FF_BLK = 256
PROJ_BLK = 512
PROMPT_T = 256
VMEM_LIMIT = 60000 * 1024


def _rmsnorm(x, g):
    ms = jnp.mean(x * x, axis=-1, keepdims=True)
    return x * lax.rsqrt(ms + EPS) * g


def _softplus(x):
    return jnp.maximum(x, 0.0) + jnp.log1p(jnp.exp(-jnp.abs(x)))


def _silu(x):
    return x * jax.nn.sigmoid(x)


def _gelu_tanh(x):
    c = math.sqrt(2.0 / math.pi)
    return x * (0.5 * (1.0 + jnp.tanh(c * (x + 0.044715 * (x * x * x)))))


def _dot(a, b):
    return jnp.dot(a, b, preferred_element_type=F32)


def _conv_taps(full_ref, rows, w, b, col):
    out = b + full_ref[CARRY_LO:CARRY_LO + rows, col] * w[0:1]
    for k in range(1, CONV_WIDTH):
        out = out + full_ref[CARRY_LO + k:CARRY_LO + k + rows, col] * w[k:k + 1]
    return out


def _run(steps):
    try:
        while True:
            next(steps)
    except StopIteration as done:
        return done.value


def _weave(main, side):
    side_result = None
    side_done = False
    try:
        while True:
            want = next(main)
            for _ in range(want):
                if not side_done:
                    try:
                        next(side)
                    except StopIteration as done:
                        side_result, side_done = done.value, True
    except StopIteration as done:
        main_result = done.value
    if not side_done:
        side_result = _run(side)
    return main_result, side_result


def _ffn_steps(load_x, gpre, gpost, wg_ref, wu_ref, wd_ref, act_ref):
    hn = _rmsnorm(load_x(), gpre).astype(BF16)
    yield
    for c in range(D_FF // FF_BLK):
        col = slice(c * FF_BLK, (c + 1) * FF_BLK)
        gate = _dot(hn, wg_ref[:, col])
        up = _dot(hn, wu_ref[:, col])
        act_ref[:, col] = (_silu(gate) * up).astype(BF16)
        yield
    parts = []
    for c in range(D_MODEL // FF_BLK):
        parts.append(_dot(act_ref[...], wd_ref[:, c * FF_BLK:(c + 1) * FF_BLK]))
        yield
    f = jnp.concatenate(parts, axis=1)
    return load_x() + _rmsnorm(f, gpost)


def _ffn(x, gpre, gpost, wg_ref, wu_ref, wd_ref, act_ref):
    return _run(_ffn_steps(lambda: x, gpre, gpost, wg_ref, wu_ref, wd_ref, act_ref))


def _lru_pre(x, gpre, win_ref):
    hn = _rmsnorm(x, gpre).astype(BF16)
    gate = _dot(hn, win_ref[:, :D_RNN])
    rec = _dot(hn, win_ref[:, D_RNN:])
    return _gelu_tanh(gate), rec


def _scan8(a, u):
    rows, ch = a.shape
    a3 = a.reshape(rows // SUBLANES, SUBLANES, ch)
    u3 = u.reshape(rows // SUBLANES, SUBLANES, ch)
    sub = lax.broadcasted_iota(jnp.int32, a3.shape, 1)
    for s in (1, 2, 4):
        m = sub >= s
        a_sh = jnp.where(m, pltpu.roll(a3, s, 1), 1.0)
        u_sh = jnp.where(m, pltpu.roll(u3, s, 1), 0.0)
        u3 = a3 * u_sh + u3
        a3 = a3 * a_sh
    return a3.reshape(rows, ch), u3.reshape(rows, ch)


def _lru_core_steps(rows, full_ref, g_ref, a_ref, u_ref, h_b, cw, cb, wr_ref, br, wi_ref, bi, lam):
    sp = _softplus(-lam)
    for k in range(LRU_BLOCKS):
        yield
        col = slice(k * LRU_BLOCK_W, (k + 1) * LRU_BLOCK_W)
        xc = _conv_taps(full_ref, rows, cw[:, col], cb[:, col], col)
        xcb = xc.astype(BF16)
        r = jax.nn.sigmoid(_dot(xcb, wr_ref[k]) + br[:, col])
        i = jax.nn.sigmoid(_dot(xcb, wi_ref[k]) + bi[:, col])
        log_a = (-LRU_C * r) * sp[:, col]
        a = jnp.exp(log_a)
        u = jnp.sqrt(-jnp.tanh(log_a) * (a * a + 1.0)) * (i * xc)
        a8, u8 = _scan8(a, u)
        a_ref[:, col] = a8
        u_ref[:, col] = u8

    yield
    for j in range(rows // SUBLANES):
        blk = slice(j * SUBLANES, (j + 1) * SUBLANES)
        hs = u_ref[blk, :] + a_ref[blk, :] * h_b
        u_ref[blk, :] = hs
        h_b = jnp.broadcast_to(hs[SUBLANES - 1:SUBLANES, :], hs.shape)
    return u_ref[...] * g_ref[...], h_b


def _lru_core(*args):
    return _run(_lru_core_steps(*args))


def _lru_post(x, gated, wout_ref, gpost):
    mix = _dot(gated.astype(BF16), wout_ref[...])
    return x + _rmsnorm(mix, gpost)


def _lru_layer_prompt_kernel(x_ref, gpre_ref, gpost_ref, win_ref, cw_ref, cb_ref, wr_ref, br_ref,
                             wi_ref, bi_ref, lam_ref, wout_ref,
                             fpre_ref, fpost_ref, wg_ref, wu_ref, wd_ref,
                             o_ref, conv_o_ref, h_o_ref,
                             full_ref, g_ref, a_ref, u_ref, h_ref, xmid_ref, act_ref):
    rows = x_ref.shape[0]
    step = pl.program_id(1)

    @pl.when(step == 0)
    def _():
        full_ref[0:CARRY_ROWS, :] = jnp.zeros((CARRY_ROWS, D_RNN), F32)
        h_ref[...] = jnp.zeros(h_ref.shape, F32)
        xmid_ref[...] = jnp.zeros(xmid_ref.shape, F32)

    def mixer():
        x = x_ref[...]
        hn = _rmsnorm(x, gpre_ref[...]).astype(BF16)
        g_ref[...] = _gelu_tanh(_dot(hn, win_ref[:, :D_RNN]))
        yield 2
        full_ref[CARRY_ROWS:CARRY_ROWS + rows, :] = _dot(hn, win_ref[:, D_RNN:])
        core = _lru_core_steps(rows, full_ref, g_ref, a_ref, u_ref, h_ref[...], cw_ref[...], cb_ref[...],
                               wr_ref, br_ref[...], wi_ref, bi_ref[...], lam_ref[...])
        pieces = 0
        try:
            while True:
                next(core)
                pieces += 1
                yield 3 if pieces <= LRU_BLOCKS else 1
        except StopIteration as done:
            gated, h_b = done.value
        h_ref[...] = h_b
        tail = full_ref[rows:rows + CARRY_ROWS, :]
        full_ref[0:CARRY_ROWS, :] = tail
        yield 1
        return _lru_post(x, gated, wout_ref, gpost_ref[...]), tail, h_b

    ffn = _ffn_steps(lambda: xmid_ref[...], fpre_ref[...], fpost_ref[...], wg_ref, wu_ref, wd_ref, act_ref)
    (x_new, tail, h_b), o_ref[...] = _weave(mixer(), ffn)
    xmid_ref[...] = x_new

    @pl.when(step == pl.num_programs(1) - 2)
    def _():
        conv_o_ref[...] = tail
        h_o_ref[...] = h_b[0:1, :]


def _lru_pre_kernel(x_ref, gpre_ref, win_ref, g_o_ref, rec_o_ref):
    g, rec = _lru_pre(x_ref[...], gpre_ref[...], win_ref)
    g_o_ref[...] = g
    rec_o_ref[...] = rec


def _lru_core_sample_kernel(g_ref, rec_ref, conv_ref, h_ref, cw_ref, cb_ref, wr_ref, br_ref,
                            wi_ref, bi_ref, lam_ref,
                            gated_o_ref, conv_o_ref, h_o_ref,
                            full_ref, a_ref, u_ref):
    rows = rec_ref.shape[0]
    full_ref[0:CARRY_ROWS, :] = jnp.zeros((CARRY_ROWS, D_RNN), F32)
    full_ref[CARRY_LO:CARRY_ROWS, :] = conv_ref[...]
    full_ref[CARRY_ROWS:CARRY_ROWS + rows, :] = rec_ref[...]
    h_b = jnp.broadcast_to(h_ref[...], (SUBLANES, D_RNN))
    gated, h_b = _lru_core(rows, full_ref, g_ref, a_ref, u_ref, h_b, cw_ref[...], cb_ref[...],
                           wr_ref, br_ref[...], wi_ref, bi_ref[...], lam_ref[...])
    gated_o_ref[...] = gated
    conv_o_ref[...] = full_ref[rows:rows + CARRY_ROWS, :]
    h_o_ref[...] = h_b[0:1, :]


def _lru_post_ffn_kernel(x_ref, gated_ref, wout_ref, gpost_ref, fpre_ref, fpost_ref,
                         wg_ref, wu_ref, wd_ref, o_ref, act_ref):
    x = _lru_post(x_ref[...], gated_ref[...], wout_ref, gpost_ref[...])
    o_ref[...] = _ffn(x, fpre_ref[...], fpost_ref[...], wg_ref, wu_ref, wd_ref, act_ref)


def _ssd_pre(x, gpre, wzx_ref, wdt_ref, dtb, z_ref, xbc_ref, xbc_row0):
    rows = x.shape[0]
    hn = _rmsnorm(x, gpre).astype(BF16)
    for c in range(D_INNER // PROJ_BLK):
        col = slice(c * PROJ_BLK, (c + 1) * PROJ_BLK)
        z_ref[:, col] = _dot(hn, wzx_ref[:, col])
    for c in range(SSD_CONV_DIM // PROJ_BLK):
        col = slice(c * PROJ_BLK, (c + 1) * PROJ_BLK)
        wcol = slice(D_INNER + c * PROJ_BLK, D_INNER + (c + 1) * PROJ_BLK)
        xbc_ref[xbc_row0:xbc_row0 + rows, col] = _dot(hn, wzx_ref[:, wcol])
    return _softplus(_dot(hn, wdt_ref[...]) + dtb)


def _ssd_conv_silu(full_ref, rows, cw, cb):
    for c in range(SSD_CONV_DIM // PROJ_BLK):
        col = slice(c * PROJ_BLK, (c + 1) * PROJ_BLK)
        out = _conv_taps(full_ref, rows, cw[:, col], cb[:, col], col)
        full_ref[CARRY_ROWS:CARRY_ROWS + rows, col] = _silu(out)


def _cumsum_rows(x):
    rows = x.shape[0]
    row = lax.broadcasted_iota(jnp.int32, x.shape, 0)
    s = 1
    while s < rows:
        x = x + jnp.where(row >= s, pltpu.roll(x, s, 0), 0.0)
        s *= 2
    return x


def _ssd_core_steps(rows, live, full_ref, dt_ref, a_neg, d_exp, st_ref, y_ref):
    q = SSD_Q
    v = live
    tril = (lax.broadcasted_iota(jnp.int32, (v, q), 0) >= lax.broadcasted_iota(jnp.int32, (v, q), 1))
    left = lax.broadcasted_iota(jnp.int32, (v, PAIR_W), 1) < SSD_HEAD_DIM
    left_q = lax.broadcasted_iota(jnp.int32, (q, PAIR_W), 1) < SSD_HEAD_DIM
    contract_last = (((1,), (1,)), ((), ()))
    contract_first = (((0,), (0,)), ((), ()))

    def pad_rows(a):
        if v == q:
            return a
        return jnp.concatenate([a, jnp.zeros((q - v,) + a.shape[1:], a.dtype)], axis=0)

    for c in range(rows // q):
        r0 = c * q
        rf = c * q + CARRY_ROWS
        dtc = dt_ref[pl.ds(r0, v), :]
        acs = _cumsum_rows(dtc * a_neg)
        acs_t = pad_rows(acs).T
        for g in range(SSD_GROUPS):
            yield
            gcol = slice(g * GROUP_W, (g + 1) * GROUP_W)
            bcol = slice(D_INNER + g * SSD_STATE, D_INNER + (g + 1) * SSD_STATE)
            ccol = slice(D_INNER + SSD_GN + g * SSD_STATE, D_INNER + SSD_GN + (g + 1) * SSD_STATE)
            b_g = full_ref[pl.ds(rf, q), bcol].astype(BF16)
            c_g = full_ref[pl.ds(rf, v), ccol].astype(BF16)
            cb = lax.dot_general(c_g, b_g, contract_last, preferred_element_type=F32)
            y_off = _dot(c_g, st_ref[:, gcol].astype(BF16))
            xdd = []
            dec = []
            for pr in range(SSD_HPG // 2):
                h1 = g * SSD_HPG + 2 * pr
                h2 = h1 + 1
                pcol = slice(h1 * SSD_HEAD_DIM, h1 * SSD_HEAD_DIM + PAIR_W)
                a1 = jnp.broadcast_to(acs[:, h1:h1 + 1], (v, q))
                a2 = jnp.broadcast_to(acs[:, h2:h2 + 1], (v, q))
                l1 = jnp.exp(jnp.where(tril, a1 - acs_t[h1:h1 + 1, :], -jnp.inf))
                l2 = jnp.exp(jnp.where(tril, a2 - acs_t[h2:h2 + 1, :], -jnp.inf))
                m = jnp.concatenate([cb * l1, cb * l2], axis=1).astype(BF16)
                a_pair = jnp.where(left, a1, a2)
                dt_pair = jnp.where(left, jnp.broadcast_to(dtc[:, h1:h1 + 1], (v, PAIR_W)),
                                    jnp.broadcast_to(dtc[:, h2:h2 + 1], (v, PAIR_W)))
                xs = full_ref[pl.ds(rf, v), pcol]
                xdt = xs * dt_pair
                xdt_q = pad_rows(xdt)
                rhs = jnp.concatenate([jnp.where(left_q, xdt_q, 0.0), jnp.where(left_q, 0.0, xdt_q)],
                                      axis=0).astype(BF16)
                y_diag = _dot(m, rhs)
                a_last = a_pair[v - 1:v, :]
                xdd.append(pad_rows((xdt * jnp.exp(a_last - a_pair)).astype(BF16)))
                dec.append(jnp.exp(a_last))
                ocol = slice(pr * PAIR_W, (pr + 1) * PAIR_W)
                y_ref[pl.ds(r0, v), pcol] = (y_diag + y_off[:, ocol] * jnp.exp(a_pair)
                                             + d_exp[:, pcol] * xs)
            states = lax.dot_general(b_g, jnp.concatenate(xdd, axis=1), contract_first,
                                     preferred_element_type=F32)
            st_ref[:, gcol] = st_ref[:, gcol] * jnp.concatenate(dec, axis=1) + states


def _ssd_core(*args):
    return _run(_ssd_core_steps(*args))


def _ssd_post_steps(x, y_ref, z_ref, normg, wout_ref, gpost, ybf_ref):
    for g in range(SSD_GROUPS):
        yield
        col = slice(g * GROUP_W, (g + 1) * GROUP_W)
        yg = y_ref[:, col] * _silu(z_ref[:, col])
        ms = jnp.mean(yg * yg, axis=-1, keepdims=True)
        ybf_ref[:, col] = (yg * lax.rsqrt(ms + EPS) * normg[:, col]).astype(BF16)
    mix = _dot(ybf_ref[...], wout_ref[...])
    return x + _rmsnorm(mix, gpost)


def _ssd_post(*args):
    return _run(_ssd_post_steps(*args))


def _ssd_layer_prompt_kernel(x_ref, gpre_ref, gpost_ref, wzx_ref, wdt_ref, dtb_ref, cw_ref, cb_ref,
                             alog_ref, dexp_ref, normg_ref, wout_ref,
                             fpre_ref, fpost_ref, wg_ref, wu_ref, wd_ref,
                             o_ref, conv_o_ref, st_o_ref,
                             full_ref, z_ref, dt_ref, y_ref, ybf_ref, st_ref, xmid_ref, act_ref):
    rows = x_ref.shape[0]
    step = pl.program_id(1)

    @pl.when(step == 0)
    def _():
        full_ref[0:CARRY_ROWS, :] = jnp.zeros((CARRY_ROWS, SSD_CONV_DIM), F32)
        st_ref[...] = jnp.zeros(st_ref.shape, F32)
        xmid_ref[...] = jnp.zeros(xmid_ref.shape, F32)

    def mixer():
        x = x_ref[...]
        hn = _rmsnorm(x, gpre_ref[...]).astype(BF16)
        dt_ref[...] = _softplus(_dot(hn, wdt_ref[...]) + dtb_ref[...])
        cw = cw_ref[...]
        cb = cb_ref[...]
        n_blk = SSD_CONV_DIM // PROJ_BLK

        def project(c):
            wcol = slice(D_INNER + c * PROJ_BLK, D_INNER + (c + 1) * PROJ_BLK)
            full_ref[CARRY_ROWS:CARRY_ROWS + rows, c * PROJ_BLK:(c + 1) * PROJ_BLK] = _dot(hn, wzx_ref[:, wcol])

        project(0)
        for c in range(n_blk):
            if c + 1 < n_blk:
                project(c + 1)
            col = slice(c * PROJ_BLK, (c + 1) * PROJ_BLK)
            tail_c = full_ref[rows:rows + CARRY_ROWS, col]
            out = _conv_taps(full_ref, rows, cw[:, col], cb[:, col], col)
            full_ref[CARRY_ROWS:CARRY_ROWS + rows, col] = _silu(out)
            full_ref[0:CARRY_ROWS, col] = tail_c
            yield 1

        z_blocks = iter(range(D_INNER // PROJ_BLK))
        core = _ssd_core_steps(rows, SSD_Q, full_ref, dt_ref, -jnp.exp(alog_ref[...]), dexp_ref[...],
                               st_ref, y_ref)
        for _ in core:
            c = next(z_blocks, None)
            if c is None:
                yield 1
            else:
                col = slice(c * PROJ_BLK, (c + 1) * PROJ_BLK)
                z_ref[:, col] = _dot(hn, wzx_ref[:, col])
                yield c % 2
        post = _ssd_post_steps(x, y_ref, z_ref, normg_ref[...], wout_ref, gpost_ref[...], ybf_ref)
        try:
            while True:
                next(post)
                yield 1
        except StopIteration as done:
            return done.value

    ffn = _ffn_steps(lambda: xmid_ref[...], fpre_ref[...], fpost_ref[...], wg_ref, wu_ref, wd_ref, act_ref)
    x_new, o_ref[...] = _weave(mixer(), ffn)
    xmid_ref[...] = x_new

    @pl.when(step == pl.num_programs(1) - 2)
    def _():
        conv_o_ref[...] = full_ref[0:CARRY_ROWS, :]
        st_o_ref[...] = st_ref[...].T


def _ssd_pre_kernel(x_ref, gpre_ref, wzx_ref, wdt_ref, dtb_ref, z_o_ref, xbc_o_ref, dt_o_ref):
    dt_o_ref[...] = _ssd_pre(x_ref[...], gpre_ref[...], wzx_ref, wdt_ref, dtb_ref[...],
                             z_o_ref, xbc_o_ref, 0)


def _ssd_core_sample_kernel(xbc_ref, dt_in_ref, conv_ref, st_in_ref, cw_ref, cb_ref, alog_ref, dexp_ref,
                            y_o_ref, conv_o_ref, st_o_ref,
                            full_ref, st_ref):
    rows = xbc_ref.shape[0]
    full_ref[...] = jnp.zeros(full_ref.shape, F32)
    full_ref[CARRY_LO:CARRY_ROWS, :] = conv_ref[...]
    full_ref[CARRY_ROWS:CARRY_ROWS + rows, :] = xbc_ref[...]
    conv_o_ref[...] = full_ref[rows:rows + CARRY_ROWS, :]
    _ssd_conv_silu(full_ref, rows, cw_ref[...], cb_ref[...])
    st_ref[...] = st_in_ref[...].T
    _ssd_core(SSD_Q, rows, full_ref, dt_in_ref, -jnp.exp(alog_ref[...]), dexp_ref[...], st_ref, y_o_ref)
    st_o_ref[...] = st_ref[...].T


def _ssd_core_sample_kernel_aliased(xbc_ref, dt_in_ref, conv_ref, st_in_ref, cw_ref, cb_ref, alog_ref,
                                    dexp_ref, st_prev_layers_ref, *rest):
    del st_prev_layers_ref
    _ssd_core_sample_kernel(xbc_ref, dt_in_ref, conv_ref, st_in_ref, cw_ref, cb_ref, alog_ref, dexp_ref, *rest)


def _ssd_post_ffn_kernel(x_ref, y_ref, z_ref, normg_ref, wout_ref, gpost_ref, fpre_ref, fpost_ref,
                         wg_ref, wu_ref, wd_ref, o_ref, ybf_ref, act_ref):
    x = _ssd_post(x_ref[...], y_ref, z_ref, normg_ref[...], wout_ref, gpost_ref[...], ybf_ref)
    o_ref[...] = _ffn(x, fpre_ref[...], fpost_ref[...], wg_ref, wu_ref, wd_ref, act_ref)


def _layer_block(stacked, idx):
    nd = stacked.ndim
    return pl.BlockSpec((None,) + stacked.shape[1:], lambda *_: (idx,) + (0,) * (nd - 1),
                        pipeline_mode=pl.Buffered(1))


def _pick(params, idx, names):
    return [(params[n], idx) for n in names]


_LRU_MIX = ('gpre', 'gpost', 'win', 'cw', 'cb', 'wr', 'br', 'wi', 'bi', 'lam', 'wout')
_LRU_CORE = ('cw', 'cb', 'wr', 'br', 'wi', 'bi', 'lam')
_SSD_MIX = ('gpre', 'gpost', 'wzx', 'wdt', 'dtb', 'cw', 'cb', 'alog', 'dexp', 'normg', 'wout')
_SSD_CORE = ('cw', 'cb', 'alog', 'dexp')
_FFN = ('gpre', 'gpost', 'wg', 'wu', 'wd')


def _params(n_axes):
    return pltpu.CompilerParams(dimension_semantics=("arbitrary",) * n_axes,
                                vmem_limit_bytes=VMEM_LIMIT)


def _prompt_layer_call(kernel, name, x, consts, state_shapes, scratch):
    bsz, seq, _ = x.shape
    t = PROMPT_T
    nt = seq // t
    return pl.pallas_call(
        kernel,
        grid=(bsz, nt + 1),
        in_specs=[pl.BlockSpec((None, t, D_MODEL), lambda b, i: (b, jnp.minimum(i, nt - 1), 0))]
        + [_layer_block(a, k) for a, k in consts],
        out_specs=[pl.BlockSpec((None, t, D_MODEL), lambda b, i: (b, jnp.maximum(i - 1, 0), 0))]
        + [pl.BlockSpec((None,) + s, lambda b, i: (b, 0, 0)) for s in state_shapes],
        out_shape=[jax.ShapeDtypeStruct((bsz, seq, D_MODEL), F32)]
        + [jax.ShapeDtypeStruct((bsz,) + s, F32) for s in state_shapes],
        scratch_shapes=scratch + [pltpu.VMEM((t, D_MODEL), F32), pltpu.VMEM((t, D_FF), BF16)],
        compiler_params=_params(2),
        name=name,
    )(x, *[a for a, _ in consts])


def _lru_prompt_call(x, lru, j, ffn, layer):
    t = PROMPT_T
    consts = _pick(lru, j, _LRU_MIX) + _pick(ffn, layer, _FFN)
    return _prompt_layer_call(
        _lru_layer_prompt_kernel, "lru_layer_prompt", x, consts,
        [(CARRY_ROWS, D_RNN), (1, D_RNN)],
        [pltpu.VMEM((t + CARRY_ROWS, D_RNN), F32),
         pltpu.VMEM((t, D_RNN), F32),
         pltpu.VMEM((t, D_RNN), F32),
         pltpu.VMEM((t, D_RNN), F32),
         pltpu.VMEM((SUBLANES, D_RNN), F32)])


def _ssd_prompt_call(x, ssd, j, ffn, layer):
    t = PROMPT_T
    consts = _pick(ssd, j, _SSD_MIX) + _pick(ffn, layer, _FFN)
    return _prompt_layer_call(
        _ssd_layer_prompt_kernel, "ssd_layer_prompt", x, consts,
        [(CARRY_ROWS, SSD_CONV_DIM), (D_INNER, SSD_STATE)],
        [pltpu.VMEM((t + CARRY_ROWS, SSD_CONV_DIM), F32),
         pltpu.VMEM((t, D_INNER), F32),
         pltpu.VMEM((t, LANES), F32),
         pltpu.VMEM((t, D_INNER), F32),
         pltpu.VMEM((t, D_INNER), BF16),
         pltpu.VMEM((SSD_STATE, D_INNER), F32)])


def _dense_call(kernel, name, row_ins, consts, out_widths, scratch):
    n = row_ins[0].shape[0]

    def rows_block(width):
        return pl.BlockSpec((n, width), lambda i: (0, 0))

    return pl.pallas_call(
        kernel,
        grid=(1,),
        in_specs=[rows_block(a.shape[1]) for a in row_ins] + [_layer_block(a, k) for a, k in consts],
        out_specs=[rows_block(w) for w in out_widths],
        out_shape=[jax.ShapeDtypeStruct((n, w), F32) for w in out_widths],
        scratch_shapes=scratch,
        compiler_params=_params(1),
        name=name,
    )(*row_ins, *[a for a, _ in consts])


def _per_seq(rows, width):
    return pl.BlockSpec((None, rows, width), lambda b: (b, 0, 0))


def _per_layer_seq(j, rows, width):
    return pl.BlockSpec((None, None, rows, width), lambda b: (j, b, 0, 0))


def _lru_core_sample_call(g, rec, conv_all, h_all, j, lru):
    bsz, seq, _ = g.shape
    consts = _pick(lru, j, _LRU_CORE)
    per_seq = _per_seq
    return pl.pallas_call(
        _lru_core_sample_kernel,
        grid=(bsz,),
        in_specs=[per_seq(seq, D_RNN), per_seq(seq, D_RNN), _per_layer_seq(j, CONV_WIDTH - 1, D_RNN),
                  _per_layer_seq(j, 1, D_RNN)] + [_layer_block(a, k) for a, k in consts],
        out_specs=[per_seq(seq, D_RNN), per_seq(CARRY_ROWS, D_RNN), per_seq(1, D_RNN)],
        out_shape=[jax.ShapeDtypeStruct((bsz, seq, D_RNN), F32),
                   jax.ShapeDtypeStruct((bsz, CARRY_ROWS, D_RNN), F32),
                   jax.ShapeDtypeStruct((bsz, 1, D_RNN), F32)],
        scratch_shapes=[pltpu.VMEM((seq + CARRY_ROWS, D_RNN), F32),
                        pltpu.VMEM((seq, D_RNN), F32),
                        pltpu.VMEM((seq, D_RNN), F32)],
        compiler_params=_params(1),
        name="lru_core_sample",
    )(g, rec, conv_all, h_all, *[a for a, _ in consts])


def _ssd_core_sample_call(xbc, dt, conv_all, st_all, j, st_new_all, ssd):
    bsz, seq, _ = xbc.shape
    consts = _pick(ssd, j, _SSD_CORE)
    per_seq = _per_seq
    layer_spec = _per_layer_seq(j, D_INNER, SSD_STATE)
    in_specs = [per_seq(seq, SSD_CONV_DIM), per_seq(seq, LANES), _per_layer_seq(j, CONV_WIDTH - 1, SSD_CONV_DIM),
                layer_spec] + [_layer_block(a, k) for a, k in consts]
    args = [xbc, dt, conv_all, st_all, *[a for a, _ in consts]]
    kernel = _ssd_core_sample_kernel
    aliases = {}
    if st_new_all is not None:
        in_specs.append(pl.BlockSpec(memory_space=pl.ANY))
        aliases = {len(args): 2}
        args.append(st_new_all)
        kernel = _ssd_core_sample_kernel_aliased
    return pl.pallas_call(
        kernel,
        grid=(bsz,),
        in_specs=in_specs,
        out_specs=[per_seq(seq, D_INNER), per_seq(CARRY_ROWS, SSD_CONV_DIM), layer_spec],
        out_shape=[jax.ShapeDtypeStruct((bsz, seq, D_INNER), F32),
                   jax.ShapeDtypeStruct((bsz, CARRY_ROWS, SSD_CONV_DIM), F32),
                   jax.ShapeDtypeStruct(st_all.shape, F32)],
        scratch_shapes=[pltpu.VMEM((SSD_Q + CARRY_ROWS, SSD_CONV_DIM), F32),
                        pltpu.VMEM((SSD_STATE, D_INNER), F32)],
        input_output_aliases=aliases,
        compiler_params=_params(1),
        name="ssd_core_sample",
    )(*args)


def _rows(v):
    return v.reshape(v.shape[0], 1, -1).astype(F32)


def _pad_lanes(v):
    return jnp.pad(v, ((0, 0),) * (v.ndim - 1) + ((0, LANES - v.shape[-1]),))


def kernel(x_prompt, x_sample, state_lru_conv, state_lru_h, state_ssd_conv, state_ssd, norm_mix_pre, norm_mix_post, norm_ffn_pre, norm_ffn_post, lru_w_in, lru_conv_w, lru_conv_b, lru_w_r, lru_b_r, lru_w_i, lru_b_i, lru_lambda, lru_w_out, ssd_w_in, ssd_conv_w, ssd_conv_b, ssd_dt_bias, ssd_a_log, ssd_d, ssd_norm, ssd_w_out, ffn_w_gate, ffn_w_up, ffn_w_down):
    n_a = lru_w_in.shape[0]
    n_b = ssd_w_in.shape[0]
    lru = dict(
        gpre=_rows(norm_mix_pre[0::2]), gpost=_rows(norm_mix_post[0::2]),
        win=lru_w_in.astype(BF16), cw=lru_conv_w, cb=_rows(lru_conv_b),
        wr=lru_w_r.astype(BF16), br=_rows(lru_b_r), wi=lru_w_i.astype(BF16), bi=_rows(lru_b_i),
        lam=_rows(lru_lambda), wout=lru_w_out.astype(BF16))
    ssd = dict(
        gpre=_rows(norm_mix_pre[1::2]), gpost=_rows(norm_mix_post[1::2]),
        wzx=ssd_w_in[:, :, :D_INNER + SSD_CONV_DIM].astype(BF16),
        wdt=_pad_lanes(ssd_w_in[:, :, D_INNER + SSD_CONV_DIM:]).astype(BF16),
        dtb=_pad_lanes(_rows(ssd_dt_bias)), cw=ssd_conv_w, cb=_rows(ssd_conv_b),
        alog=_pad_lanes(_rows(ssd_a_log)), dexp=_rows(jnp.repeat(ssd_d, SSD_HEAD_DIM, axis=1)),
        normg=_rows(ssd_norm), wout=ssd_w_out.astype(BF16))
    ffn = dict(
        gpre=_rows(norm_ffn_pre), gpost=_rows(norm_ffn_post),
        wg=ffn_w_gate.astype(BF16), wu=ffn_w_up.astype(BF16), wd=ffn_w_down.astype(BF16))

    bp, seq, _ = x_prompt.shape
    x = x_prompt
    p_lc, p_lh, p_sc, p_ss = [], [], [], []
    for layer in range(DEPTH):
        j = layer // 2
        if layer % 2 == 0:
            x, conv, h = _lru_prompt_call(x, lru, j, ffn, layer)
            p_lc.append(conv[:, CARRY_LO:])
            p_lh.append(h[:, 0])
        else:
            x, conv, st = _ssd_prompt_call(x, ssd, j, ffn, layer)
            p_sc.append(conv[:, CARRY_LO:])
            p_ss.append(st.reshape(bp, SSD_HEADS, SSD_HEAD_DIM, SSD_STATE))
    y_prompt = x

    bs, sl, _ = x_sample.shape
    n = bs * sl
    xs = x_sample.reshape(n, D_MODEL)
    s_lc, s_lh, s_sc = [], [], []
    s_ss_all = None
    for layer in range(DEPTH):
        j = layer // 2
        if layer % 2 == 0:
            g, rec = _dense_call(_lru_pre_kernel, "lru_pre", (xs,), _pick(lru, j, ('gpre', 'win')),
                                 (D_RNN, D_RNN), [])
            gated, conv, h = _lru_core_sample_call(
                g.reshape(bs, sl, D_RNN), rec.reshape(bs, sl, D_RNN), state_lru_conv,
                state_lru_h.reshape(n_a, bs, 1, D_RNN), j, lru)
            s_lc.append(conv[:, CARRY_LO:])
            s_lh.append(h[:, 0])
            (xs,) = _dense_call(
                _lru_post_ffn_kernel, "lru_post_ffn", (xs, gated.reshape(n, D_RNN)),
                _pick(lru, j, ('wout', 'gpost')) + _pick(ffn, layer, _FFN),
                (D_MODEL,), [pltpu.VMEM((n, D_FF), BF16)])
        else:
            z, xbc, dt = _dense_call(_ssd_pre_kernel, "ssd_pre", (xs,),
                                     _pick(ssd, j, ('gpre', 'wzx', 'wdt', 'dtb')),
                                     (D_INNER, SSD_CONV_DIM, LANES), [])
            y, conv, s_ss_all = _ssd_core_sample_call(
                xbc.reshape(bs, sl, SSD_CONV_DIM), dt.reshape(bs, sl, LANES), state_ssd_conv,
                state_ssd.reshape(n_b, bs, D_INNER, SSD_STATE), j, s_ss_all, ssd)
            s_sc.append(conv[:, CARRY_LO:])
            (xs,) = _dense_call(
                _ssd_post_ffn_kernel, "ssd_post_ffn", (xs, y.reshape(n, D_INNER), z),
                _pick(ssd, j, ('normg', 'wout', 'gpost')) + _pick(ffn, layer, _FFN),
                (D_MODEL,), [pltpu.VMEM((n, D_INNER), BF16), pltpu.VMEM((n, D_FF), BF16)])
    y_sample = xs.reshape(bs, sl, D_MODEL)

    return (y_prompt, y_sample,
            jnp.stack(p_lc), jnp.stack(p_lh), jnp.stack(p_sc), jnp.stack(p_ss),
            jnp.stack(s_lc), jnp.stack(s_lh), jnp.stack(s_sc),
            s_ss_all.reshape(n_b, bs, SSD_HEADS, SSD_HEAD_DIM, SSD_STATE))
```

```python
import functools
import math

import jax
import jax.numpy as jnp
from jax import lax
from jax.experimental import pallas as pl
from jax.experimental.pallas import tpu as pltpu

F32 = jnp.float32
BF16 = jnp.bfloat16

D_MODEL = 1024
DEPTH = 4
CONV_WIDTH = 4
EPS = 1e-6
D_RNN = D_MODEL
LRU_BLOCKS = 4
LRU_BLOCK_W = D_RNN // LRU_BLOCKS
LRU_C = 8.0
D_INNER = 2 * D_MODEL
SSD_HEAD_DIM = 64
SSD_HEADS = D_INNER // SSD_HEAD_DIM
SSD_GROUPS = 4
SSD_HPG = SSD_HEADS // SSD_GROUPS
SSD_STATE = 128
SSD_GN = SSD_GROUPS * SSD_STATE
SSD_CONV_DIM = D_INNER + 2 * SSD_GN
D_FF = -(-8 * D_MODEL // (3 * 256)) * 256

LANES = 128
SUBLANES = 8
CARRY_ROWS = SUBLANES
CARRY_LO = CARRY_ROWS - (CONV_WIDTH - 1)
SSD_Q = 128
GROUP_W = D_INNER // SSD_GROUPS
PAIR_W = 2 * SSD_HEAD_DIM
FF_BLK = 256
PROJ_BLK = 512
LRU_T = 512
SSD_T = 256
VMEM_LIMIT = 60000 * 1024


def _rmsnorm(x, g):
    ms = jnp.mean(x * x, axis=-1, keepdims=True)
    return x * lax.rsqrt(ms + EPS) * g


def _softplus(x):
    return jnp.maximum(x, 0.0) + jnp.log1p(jnp.exp(-jnp.abs(x)))


def _silu(x):
    return x * jax.nn.sigmoid(x)


def _gelu_tanh(x):
    c = math.sqrt(2.0 / math.pi)
    return x * (0.5 * (1.0 + jnp.tanh(c * (x + 0.044715 * (x * x * x)))))


def _dot(a, b):
    return jnp.dot(a, b, preferred_element_type=F32)


def _conv_taps(full_ref, rows, w, b, col):
    out = b + full_ref[CARRY_LO:CARRY_LO + rows, col] * w[0:1]
    for k in range(1, CONV_WIDTH):
        out = out + full_ref[CARRY_LO + k:CARRY_LO + k + rows, col] * w[k:k + 1]
    return out


def _run(steps):
    try:
        while True:
            next(steps)
    except StopIteration as done:
        return done.value


def _weave(main, side, lead):
    side_result = None
    side_done = False
    try:
        want = lead
        while True:
            for _ in range(want):
                if not side_done:
                    try:
                        next(side)
                    except StopIteration as done:
                        side_result, side_done = done.value, True
            want = next(main)
    except StopIteration as done:
        main_result = done.value
    if not side_done:
        side_result = _run(side)
    return main_result, side_result


def _ffn_steps(load_x, gpre, gpost, wg_ref, wu_ref, wd_ref, act_ref):
    hn = _rmsnorm(load_x(), gpre).astype(BF16)
    yield
    for c in range(D_FF // FF_BLK):
        col = slice(c * FF_BLK, (c + 1) * FF_BLK)
        gate = _dot(hn, wg_ref[:, col])
        up = _dot(hn, wu_ref[:, col])
        act_ref[:, col] = (_silu(gate) * up).astype(BF16)
        yield
    parts = []
    for c in range(D_MODEL // FF_BLK):
        parts.append(_dot(act_ref[...], wd_ref[:, c * FF_BLK:(c + 1) * FF_BLK]))
        yield
    f = jnp.concatenate(parts, axis=1)
    return load_x() + _rmsnorm(f, gpost)


def _ffn(x, gpre, gpost, wg_ref, wu_ref, wd_ref, act_ref):
    return _run(_ffn_steps(lambda: x, gpre, gpost, wg_ref, wu_ref, wd_ref, act_ref))


def _lru_pre(x, gpre, win_ref):
    hn = _rmsnorm(x, gpre).astype(BF16)
    gate = _dot(hn, win_ref[:, :D_RNN])
    rec = _dot(hn, win_ref[:, D_RNN:])
    return _gelu_tanh(gate), rec


def _scan8(a, u):
    rows, ch = a.shape
    a3 = a.reshape(rows // SUBLANES, SUBLANES, ch)
    u3 = u.reshape(rows // SUBLANES, SUBLANES, ch)
    sub = lax.broadcasted_iota(jnp.int32, a3.shape, 1)
    for s in (1, 2, 4):
        m = sub >= s
        a_sh = jnp.where(m, pltpu.roll(a3, s, 1), 1.0)
        u_sh = jnp.where(m, pltpu.roll(u3, s, 1), 0.0)
        u3 = a3 * u_sh + u3
        a3 = a3 * a_sh
    return a3.reshape(rows, ch), u3.reshape(rows, ch)


def _lru_core_steps(rows, full_ref, g_ref, a_ref, u_ref, h_b, cw, cb, wr_ref, br, wi_ref, bi, lam):
    sp = _softplus(-lam)
    for k in range(LRU_BLOCKS):
        yield
        col = slice(k * LRU_BLOCK_W, (k + 1) * LRU_BLOCK_W)
        xc = _conv_taps(full_ref, rows, cw[:, col], cb[:, col], col)
        xcb = xc.astype(BF16)
        r = jax.nn.sigmoid(_dot(xcb, wr_ref[k]) + br[:, col])
        i = jax.nn.sigmoid(_dot(xcb, wi_ref[k]) + bi[:, col])
        log_a = (-LRU_C * r) * sp[:, col]
        a = jnp.exp(log_a)
        u = jnp.sqrt(-jnp.tanh(log_a) * (a * a + 1.0)) * (i * xc)
        a8, u8 = _scan8(a, u)
        a_ref[:, col] = a8
        u_ref[:, col] = u8

    yield
    for j in range(rows // SUBLANES):
        blk = slice(j * SUBLANES, (j + 1) * SUBLANES)
        hs = u_ref[blk, :] + a_ref[blk, :] * h_b
        u_ref[blk, :] = hs
        h_b = jnp.broadcast_to(hs[SUBLANES - 1:SUBLANES, :], hs.shape)
    return u_ref[...] * g_ref[...], h_b


def _lru_core(*args):
    return _run(_lru_core_steps(*args))


def _lru_post(x, gated, wout_ref, gpost):
    mix = _dot(gated.astype(BF16), wout_ref[...])
    return x + _rmsnorm(mix, gpost)


def _lru_layer_prompt_kernel(x_ref, gpre_ref, gpost_ref, win_ref, cw_ref, cb_ref, wr_ref, br_ref,
                             wi_ref, bi_ref, lam_ref, wout_ref,
                             fpre_ref, fpost_ref, wg_ref, wu_ref, wd_ref,
                             o_ref, conv_o_ref, h_o_ref,
                             full_ref, g_ref, a_ref, u_ref, h_ref, xmid_ref, act_ref):
    rows = x_ref.shape[0]
    step = pl.program_id(1)

    @pl.when(step == 0)
    def _():
        full_ref[0:CARRY_ROWS, :] = jnp.zeros((CARRY_ROWS, D_RNN), F32)
        h_ref[...] = jnp.zeros(h_ref.shape, F32)
        xmid_ref[...] = jnp.zeros(xmid_ref.shape, F32)

    def mixer():
        x = x_ref[...]
        hn = _rmsnorm(x, gpre_ref[...]).astype(BF16)
        g_ref[...] = _gelu_tanh(_dot(hn, win_ref[:, :D_RNN]))
        yield 1
        full_ref[CARRY_ROWS:CARRY_ROWS + rows, :] = _dot(hn, win_ref[:, D_RNN:])
        core = _lru_core_steps(rows, full_ref, g_ref, a_ref, u_ref, h_ref[...], cw_ref[...], cb_ref[...],
                               wr_ref, br_ref[...], wi_ref, bi_ref[...], lam_ref[...])
        pieces = 0
        try:
            while True:
                next(core)
                pieces += 1
                yield 3 if pieces <= LRU_BLOCKS else 1
        except StopIteration as done:
            gated, h_b = done.value
        h_ref[...] = h_b
        tail = full_ref[rows:rows + CARRY_ROWS, :]
        full_ref[0:CARRY_ROWS, :] = tail
        yield 1
        return _lru_post(x, gated, wout_ref, gpost_ref[...]), tail, h_b

    ffn = _ffn_steps(lambda: xmid_ref[...], fpre_ref[...], fpost_ref[...], wg_ref, wu_ref, wd_ref, act_ref)
    (x_new, tail, h_b), o_ref[...] = _weave(mixer(), ffn, lead=2)
    xmid_ref[...] = x_new

    @pl.when(step == pl.num_programs(1) - 2)
    def _():
        conv_o_ref[...] = tail
        h_o_ref[...] = h_b[0:1, :]


def _lru_pre_kernel(x_ref, gpre_ref, win_ref, g_o_ref, rec_o_ref):
    g, rec = _lru_pre(x_ref[...], gpre_ref[...], win_ref)
    g_o_ref[...] = g
    rec_o_ref[...] = rec


def _lru_core_sample_kernel(g_ref, rec_ref, conv_ref, h_ref, cw_ref, cb_ref, wr_ref, br_ref,
                            wi_ref, bi_ref, lam_ref,
                            gated_o_ref, conv_o_ref, h_o_ref,
                            full_ref, a_ref, u_ref):
    rows = rec_ref.shape[0]
    full_ref[0:CARRY_ROWS, :] = jnp.zeros((CARRY_ROWS, D_RNN), F32)
    full_ref[CARRY_LO:CARRY_ROWS, :] = conv_ref[...]
    full_ref[CARRY_ROWS:CARRY_ROWS + rows, :] = rec_ref[...]
    h_b = jnp.broadcast_to(h_ref[...], (SUBLANES, D_RNN))
    gated, h_b = _lru_core(rows, full_ref, g_ref, a_ref, u_ref, h_b, cw_ref[...], cb_ref[...],
                           wr_ref, br_ref[...], wi_ref, bi_ref[...], lam_ref[...])
    gated_o_ref[...] = gated
    conv_o_ref[...] = full_ref[rows:rows + CARRY_ROWS, :]
    h_o_ref[...] = h_b[0:1, :]


def _lru_post_ffn_kernel(x_ref, gated_ref, wout_ref, gpost_ref, fpre_ref, fpost_ref,
                         wg_ref, wu_ref, wd_ref, o_ref, act_ref):
    x = _lru_post(x_ref[...], gated_ref[...], wout_ref, gpost_ref[...])
    o_ref[...] = _ffn(x, fpre_ref[...], fpost_ref[...], wg_ref, wu_ref, wd_ref, act_ref)


def _ssd_pre(x, gpre, wzx_ref, wdt_ref, dtb, z_ref, xbc_ref, xbc_row0):
    rows = x.shape[0]
    hn = _rmsnorm(x, gpre).astype(BF16)
    for c in range(D_INNER // PROJ_BLK):
        col = slice(c * PROJ_BLK, (c + 1) * PROJ_BLK)
        z_ref[:, col] = _dot(hn, wzx_ref[:, col])
    for c in range(SSD_CONV_DIM // PROJ_BLK):
        col = slice(c * PROJ_BLK, (c + 1) * PROJ_BLK)
        wcol = slice(D_INNER + c * PROJ_BLK, D_INNER + (c + 1) * PROJ_BLK)
        xbc_ref[xbc_row0:xbc_row0 + rows, col] = _dot(hn, wzx_ref[:, wcol])
    return _softplus(_dot(hn, wdt_ref[...]) + dtb)


def _ssd_conv_silu(full_ref, rows, cw, cb):
    for c in range(SSD_CONV_DIM // PROJ_BLK):
        col = slice(c * PROJ_BLK, (c + 1) * PROJ_BLK)
        out = _conv_taps(full_ref, rows, cw[:, col], cb[:, col], col)
        full_ref[CARRY_ROWS:CARRY_ROWS + rows, col] = _silu(out)


def _cumsum_rows(x):
    rows = x.shape[0]
    row = lax.broadcasted_iota(jnp.int32, x.shape, 0)
    s = 1
    while s < rows:
        x = x + jnp.where(row >= s, pltpu.roll(x, s, 0), 0.0)
        s *= 2
    return x


def _ssd_core_steps(rows, live, full_ref, dt_ref, a_neg, d_exp, st_ref, y_ref):
    q = SSD_Q
    v = live
    tril = (lax.broadcasted_iota(jnp.int32, (v, q), 0) >= lax.broadcasted_iota(jnp.int32, (v, q), 1))
    left = lax.broadcasted_iota(jnp.int32, (v, PAIR_W), 1) < SSD_HEAD_DIM
    left_q = lax.broadcasted_iota(jnp.int32, (q, PAIR_W), 1) < SSD_HEAD_DIM
    contract_last = (((1,), (1,)), ((), ()))
    contract_first = (((0,), (0,)), ((), ()))

    def pad_rows(a):
        if v == q:
            return a
        return jnp.concatenate([a, jnp.zeros((q - v,) + a.shape[1:], a.dtype)], axis=0)

    for c in range(rows // q):
        r0 = c * q
        rf = c * q + CARRY_ROWS
        dtc = dt_ref[pl.ds(r0, v), :]
        acs = _cumsum_rows(dtc * a_neg)
        acs_t = pad_rows(acs).T
        for g in range(SSD_GROUPS):
            yield
            gcol = slice(g * GROUP_W, (g + 1) * GROUP_W)
            bcol = slice(D_INNER + g * SSD_STATE, D_INNER + (g + 1) * SSD_STATE)
            ccol = slice(D_INNER + SSD_GN + g * SSD_STATE, D_INNER + SSD_GN + (g + 1) * SSD_STATE)
            b_g = full_ref[pl.ds(rf, q), bcol].astype(BF16)
            c_g = full_ref[pl.ds(rf, v), ccol].astype(BF16)
            cb = lax.dot_general(c_g, b_g, contract_last, preferred_element_type=F32)
            y_off = _dot(c_g, st_ref[:, gcol].astype(BF16))
            xdd = []
            dec = []
            for pr in range(SSD_HPG // 2):
                h1 = g * SSD_HPG + 2 * pr
                h2 = h1 + 1
                pcol = slice(h1 * SSD_HEAD_DIM, h1 * SSD_HEAD_DIM + PAIR_W)
                a1 = jnp.broadcast_to(acs[:, h1:h1 + 1], (v, q))
                a2 = jnp.broadcast_to(acs[:, h2:h2 + 1], (v, q))
                l1 = jnp.exp(jnp.where(tril, a1 - acs_t[h1:h1 + 1, :], -jnp.inf))
                l2 = jnp.exp(jnp.where(tril, a2 - acs_t[h2:h2 + 1, :], -jnp.inf))
                m = jnp.concatenate([cb * l1, cb * l2], axis=1).astype(BF16)
                a_pair = jnp.where(left, a1, a2)
                dt_pair = jnp.where(left, jnp.broadcast_to(dtc[:, h1:h1 + 1], (v, PAIR_W)),
                                    jnp.broadcast_to(dtc[:, h2:h2 + 1], (v, PAIR_W)))
                xs = full_ref[pl.ds(rf, v), pcol]
                xdt = xs * dt_pair
                xdt_q = pad_rows(xdt)
                rhs = jnp.concatenate([jnp.where(left_q, xdt_q, 0.0), jnp.where(left_q, 0.0, xdt_q)],
                                      axis=0).astype(BF16)
                y_diag = _dot(m, rhs)
                a_last = a_pair[v - 1:v, :]
                xdd.append(pad_rows((xdt * jnp.exp(a_last - a_pair)).astype(BF16)))
                dec.append(jnp.exp(a_last))
                ocol = slice(pr * PAIR_W, (pr + 1) * PAIR_W)
                y_ref[pl.ds(r0, v), pcol] = (y_diag + y_off[:, ocol] * jnp.exp(a_pair)
                                             + d_exp[:, pcol] * xs)
            states = lax.dot_general(b_g, jnp.concatenate(xdd, axis=1), contract_first,
                                     preferred_element_type=F32)
            st_ref[:, gcol] = st_ref[:, gcol] * jnp.concatenate(dec, axis=1) + states


def _ssd_core(*args):
    return _run(_ssd_core_steps(*args))


def _ssd_post_steps(x, y_ref, z_ref, normg, wout_ref, gpost, ybf_ref):
    for g in range(SSD_GROUPS):
        yield
        col = slice(g * GROUP_W, (g + 1) * GROUP_W)
        yg = y_ref[:, col] * _silu(z_ref[:, col])
        ms = jnp.mean(yg * yg, axis=-1, keepdims=True)
        ybf_ref[:, col] = (yg * lax.rsqrt(ms + EPS) * normg[:, col]).astype(BF16)
    mix = _dot(ybf_ref[...], wout_ref[...])
    return x + _rmsnorm(mix, gpost)


def _ssd_post(*args):
    return _run(_ssd_post_steps(*args))


def _ssd_layer_prompt_kernel(x_ref, gpre_ref, gpost_ref, wzx_ref, wdt_ref, dtb_ref, cw_ref, cb_ref,
                             alog_ref, dexp_ref, normg_ref, wout_ref,
                             fpre_ref, fpost_ref, wg_ref, wu_ref, wd_ref,
                             o_ref, conv_o_ref, st_o_ref,
                             full_ref, z_ref, dt_ref, y_ref, ybf_ref, st_ref, xmid_ref, act_ref):
    rows = x_ref.shape[0]
    step = pl.program_id(1)

    @pl.when(step == 0)
    def _():
        full_ref[0:CARRY_ROWS, :] = jnp.zeros((CARRY_ROWS, SSD_CONV_DIM), F32)
        st_ref[...] = jnp.zeros(st_ref.shape, F32)
        xmid_ref[...] = jnp.zeros(xmid_ref.shape, F32)

    def mixer():
        x = x_ref[...]
        hn = _rmsnorm(x, gpre_ref[...]).astype(BF16)
        dt_ref[...] = _softplus(_dot(hn, wdt_ref[...]) + dtb_ref[...])
        cw = cw_ref[...]
        cb = cb_ref[...]
        n_blk = SSD_CONV_DIM // PROJ_BLK

        def project(c):
            wcol = slice(D_INNER + c * PROJ_BLK, D_INNER + (c + 1) * PROJ_BLK)
            full_ref[CARRY_ROWS:CARRY_ROWS + rows, c * PROJ_BLK:(c + 1) * PROJ_BLK] = _dot(hn, wzx_ref[:, wcol])

        project(0)
        for c in range(n_blk):
            if c + 1 < n_blk:
                project(c + 1)
            col = slice(c * PROJ_BLK, (c + 1) * PROJ_BLK)
            tail_c = full_ref[rows:rows + CARRY_ROWS, col]
            out = _conv_taps(full_ref, rows, cw[:, col], cb[:, col], col)
            full_ref[CARRY_ROWS:CARRY_ROWS + rows, col] = _silu(out)
            full_ref[0:CARRY_ROWS, col] = tail_c
            yield 1

        z_blocks = iter(range(D_INNER // PROJ_BLK))
        core = _ssd_core_steps(rows, SSD_Q, full_ref, dt_ref, -jnp.exp(alog_ref[...]), dexp_ref[...],
                               st_ref, y_ref)
        for _ in core:
            c = next(z_blocks, None)
            if c is None:
                yield 1
            else:
                col = slice(c * PROJ_BLK, (c + 1) * PROJ_BLK)
                z_ref[:, col] = _dot(hn, wzx_ref[:, col])
                yield c % 2
        post = _ssd_post_steps(x, y_ref, z_ref, normg_ref[...], wout_ref, gpost_ref[...], ybf_ref)
        try:
            while True:
                next(post)
                yield 1
        except StopIteration as done:
            return done.value

    ffn = _ffn_steps(lambda: xmid_ref[...], fpre_ref[...], fpost_ref[...], wg_ref, wu_ref, wd_ref, act_ref)
    x_new, o_ref[...] = _weave(mixer(), ffn, lead=2)
    xmid_ref[...] = x_new

    @pl.when(step == pl.num_programs(1) - 2)
    def _():
        conv_o_ref[...] = full_ref[0:CARRY_ROWS, :]
        st_o_ref[...] = st_ref[...].T


def _ssd_pre_kernel(x_ref, gpre_ref, wzx_ref, wdt_ref, dtb_ref, z_o_ref, xbc_o_ref, dt_o_ref):
    dt_o_ref[...] = _ssd_pre(x_ref[...], gpre_ref[...], wzx_ref, wdt_ref, dtb_ref[...],
                             z_o_ref, xbc_o_ref, 0)


def _ssd_core_sample_kernel(xbc_ref, dt_in_ref, conv_ref, st_in_ref, cw_ref, cb_ref, alog_ref, dexp_ref,
                            y_o_ref, conv_o_ref, st_o_ref,
                            full_ref, st_ref):
    rows = xbc_ref.shape[0]
    full_ref[...] = jnp.zeros(full_ref.shape, F32)
    full_ref[CARRY_LO:CARRY_ROWS, :] = conv_ref[...]
    full_ref[CARRY_ROWS:CARRY_ROWS + rows, :] = xbc_ref[...]
    conv_o_ref[...] = full_ref[rows:rows + CARRY_ROWS, :]
    _ssd_conv_silu(full_ref, rows, cw_ref[...], cb_ref[...])
    st_ref[...] = st_in_ref[...].T
    _ssd_core(SSD_Q, rows, full_ref, dt_in_ref, -jnp.exp(alog_ref[...]), dexp_ref[...], st_ref, y_o_ref)
    st_o_ref[...] = st_ref[...].T


def _ssd_core_sample_kernel_aliased(xbc_ref, dt_in_ref, conv_ref, st_in_ref, cw_ref, cb_ref, alog_ref,
                                    dexp_ref, st_prev_layers_ref, *rest):
    del st_prev_layers_ref
    _ssd_core_sample_kernel(xbc_ref, dt_in_ref, conv_ref, st_in_ref, cw_ref, cb_ref, alog_ref, dexp_ref, *rest)


def _ssd_post_ffn_kernel(x_ref, y_ref, z_ref, normg_ref, wout_ref, gpost_ref, fpre_ref, fpost_ref,
                         wg_ref, wu_ref, wd_ref, o_ref, ybf_ref, act_ref):
    x = _ssd_post(x_ref[...], y_ref, z_ref, normg_ref[...], wout_ref, gpost_ref[...], ybf_ref)
    o_ref[...] = _ffn(x, fpre_ref[...], fpost_ref[...], wg_ref, wu_ref, wd_ref, act_ref)


def _layer_block(stacked, idx):
    nd = stacked.ndim
    return pl.BlockSpec((None,) + stacked.shape[1:], lambda *_: (idx,) + (0,) * (nd - 1),
                        pipeline_mode=pl.Buffered(1))


def _pick(params, idx, names):
    return [(params[n], idx) for n in names]


_LRU_MIX = ('gpre', 'gpost', 'win', 'cw', 'cb', 'wr', 'br', 'wi', 'bi', 'lam', 'wout')
_LRU_CORE = ('cw', 'cb', 'wr', 'br', 'wi', 'bi', 'lam')
_SSD_MIX = ('gpre', 'gpost', 'wzx', 'wdt', 'dtb', 'cw', 'cb', 'alog', 'dexp', 'normg', 'wout')
_SSD_CORE = ('cw', 'cb', 'alog', 'dexp')
_FFN = ('gpre', 'gpost', 'wg', 'wu', 'wd')


def _params(n_axes):
    return pltpu.CompilerParams(dimension_semantics=("arbitrary",) * n_axes,
                                vmem_limit_bytes=VMEM_LIMIT)


def _prompt_layer_call(kernel, name, x, t, consts, state_shapes, scratch):
    bsz, seq, _ = x.shape
    nt = seq // t
    return pl.pallas_call(
        kernel,
        grid=(bsz, nt + 1),
        in_specs=[pl.BlockSpec((None, t, D_MODEL), lambda b, i: (b, jnp.minimum(i, nt - 1), 0))]
        + [_layer_block(a, k) for a, k in consts],
        out_specs=[pl.BlockSpec((None, t, D_MODEL), lambda b, i: (b, jnp.maximum(i - 1, 0), 0))]
        + [pl.BlockSpec((None,) + s, lambda b, i: (b, 0, 0)) for s in state_shapes],
        out_shape=[jax.ShapeDtypeStruct((bsz, seq, D_MODEL), F32)]
        + [jax.ShapeDtypeStruct((bsz,) + s, F32) for s in state_shapes],
        scratch_shapes=scratch + [pltpu.VMEM((t, D_MODEL), F32), pltpu.VMEM((t, D_FF), BF16)],
        compiler_params=_params(2),
        name=name,
    )(x, *[a for a, _ in consts])


def _lru_prompt_call(x, lru, j, ffn, layer):
    t = LRU_T
    consts = _pick(lru, j, _LRU_MIX) + _pick(ffn, layer, _FFN)
    return _prompt_layer_call(
        _lru_layer_prompt_kernel, "lru_layer_prompt", x, t, consts,
        [(CARRY_ROWS, D_RNN), (1, D_RNN)],
        [pltpu.VMEM((t + CARRY_ROWS, D_RNN), F32),
         pltpu.VMEM((t, D_RNN), F32),
         pltpu.VMEM((t, D_RNN), F32),
         pltpu.VMEM((t, D_RNN), F32),
         pltpu.VMEM((SUBLANES, D_RNN), F32)])


def _ssd_prompt_call(x, ssd, j, ffn, layer):
    t = SSD_T
    consts = _pick(ssd, j, _SSD_MIX) + _pick(ffn, layer, _FFN)
    return _prompt_layer_call(
        _ssd_layer_prompt_kernel, "ssd_layer_prompt", x, t, consts,
        [(CARRY_ROWS, SSD_CONV_DIM), (D_INNER, SSD_STATE)],
        [pltpu.VMEM((t + CARRY_ROWS, SSD_CONV_DIM), F32),
         pltpu.VMEM((t, D_INNER), F32),
         pltpu.VMEM((t, LANES), F32),
         pltpu.VMEM((t, D_INNER), F32),
         pltpu.VMEM((t, D_INNER), BF16),
         pltpu.VMEM((SSD_STATE, D_INNER), F32)])


def _dense_call(kernel, name, row_ins, consts, out_widths, scratch):
    n = row_ins[0].shape[0]

    def rows_block(width):
        return pl.BlockSpec((n, width), lambda i: (0, 0))

    return pl.pallas_call(
        kernel,
        grid=(1,),
        in_specs=[rows_block(a.shape[1]) for a in row_ins] + [_layer_block(a, k) for a, k in consts],
        out_specs=[rows_block(w) for w in out_widths],
        out_shape=[jax.ShapeDtypeStruct((n, w), F32) for w in out_widths],
        scratch_shapes=scratch,
        compiler_params=_params(1),
        name=name,
    )(*row_ins, *[a for a, _ in consts])


def _per_seq(rows, width):
    return pl.BlockSpec((None, rows, width), lambda b: (b, 0, 0))


def _per_layer_seq(j, rows, width):
    return pl.BlockSpec((None, None, rows, width), lambda b: (j, b, 0, 0))


def _lru_core_sample_call(g, rec, conv_all, h_all, j, lru):
    bsz, seq, _ = g.shape
    consts = _pick(lru, j, _LRU_CORE)
    per_seq = _per_seq
    return pl.pallas_call(
        _lru_core_sample_kernel,
        grid=(bsz,),
        in_specs=[per_seq(seq, D_RNN), per_seq(seq, D_RNN), _per_layer_seq(j, CONV_WIDTH - 1, D_RNN),
                  _per_layer_seq(j, 1, D_RNN)] + [_layer_block(a, k) for a, k in consts],
        out_specs=[per_seq(seq, D_RNN), per_seq(CARRY_ROWS, D_RNN), per_seq(1, D_RNN)],
        out_shape=[jax.ShapeDtypeStruct((bsz, seq, D_RNN), F32),
                   jax.ShapeDtypeStruct((bsz, CARRY_ROWS, D_RNN), F32),
                   jax.ShapeDtypeStruct((bsz, 1, D_RNN), F32)],
        scratch_shapes=[pltpu.VMEM((seq + CARRY_ROWS, D_RNN), F32),
                        pltpu.VMEM((seq, D_RNN), F32),
                        pltpu.VMEM((seq, D_RNN), F32)],
        compiler_params=_params(1),
        name="lru_core_sample",
    )(g, rec, conv_all, h_all, *[a for a, _ in consts])


def _ssd_core_sample_call(xbc, dt, conv_all, st_all, j, st_new_all, ssd):
    bsz, seq, _ = xbc.shape
    consts = _pick(ssd, j, _SSD_CORE)
    per_seq = _per_seq
    layer_spec = _per_layer_seq(j, D_INNER, SSD_STATE)
    in_specs = [per_seq(seq, SSD_CONV_DIM), per_seq(seq, LANES), _per_layer_seq(j, CONV_WIDTH - 1, SSD_CONV_DIM),
                layer_spec] + [_layer_block(a, k) for a, k in consts]
    args = [xbc, dt, conv_all, st_all, *[a for a, _ in consts]]
    kernel = _ssd_core_sample_kernel
    aliases = {}
    if st_new_all is not None:
        in_specs.append(pl.BlockSpec(memory_space=pl.ANY))
        aliases = {len(args): 2}
        args.append(st_new_all)
        kernel = _ssd_core_sample_kernel_aliased
    return pl.pallas_call(
        kernel,
        grid=(bsz,),
        in_specs=in_specs,
        out_specs=[per_seq(seq, D_INNER), per_seq(CARRY_ROWS, SSD_CONV_DIM), layer_spec],
        out_shape=[jax.ShapeDtypeStruct((bsz, seq, D_INNER), F32),
                   jax.ShapeDtypeStruct((bsz, CARRY_ROWS, SSD_CONV_DIM), F32),
                   jax.ShapeDtypeStruct(st_all.shape, F32)],
        scratch_shapes=[pltpu.VMEM((SSD_Q + CARRY_ROWS, SSD_CONV_DIM), F32),
                        pltpu.VMEM((SSD_STATE, D_INNER), F32)],
        input_output_aliases=aliases,
        compiler_params=_params(1),
        name="ssd_core_sample",
    )(*args)


def _rows(v):
    return v.reshape(v.shape[0], 1, -1).astype(F32)


def _pad_lanes(v):
    return jnp.pad(v, ((0, 0),) * (v.ndim - 1) + ((0, LANES - v.shape[-1]),))


def kernel(x_prompt, x_sample, state_lru_conv, state_lru_h, state_ssd_conv, state_ssd, norm_mix_pre, norm_mix_post, norm_ffn_pre, norm_ffn_post, lru_w_in, lru_conv_w, lru_conv_b, lru_w_r, lru_b_r, lru_w_i, lru_b_i, lru_lambda, lru_w_out, ssd_w_in, ssd_conv_w, ssd_conv_b, ssd_dt_bias, ssd_a_log, ssd_d, ssd_norm, ssd_w_out, ffn_w_gate, ffn_w_up, ffn_w_down):
    n_a = lru_w_in.shape[0]
    n_b = ssd_w_in.shape[0]
    lru = dict(
        gpre=_rows(norm_mix_pre[0::2]), gpost=_rows(norm_mix_post[0::2]),
        win=lru_w_in.astype(BF16), cw=lru_conv_w, cb=_rows(lru_conv_b),
        wr=lru_w_r.astype(BF16), br=_rows(lru_b_r), wi=lru_w_i.astype(BF16), bi=_rows(lru_b_i),
        lam=_rows(lru_lambda), wout=lru_w_out.astype(BF16))
    ssd = dict(
        gpre=_rows(norm_mix_pre[1::2]), gpost=_rows(norm_mix_post[1::2]),
        wzx=ssd_w_in.astype(BF16),
        wdt=_pad_lanes(ssd_w_in[:, :, D_INNER + SSD_CONV_DIM:]).astype(BF16),
        dtb=_pad_lanes(_rows(ssd_dt_bias)), cw=ssd_conv_w, cb=_rows(ssd_conv_b),
        alog=_pad_lanes(_rows(ssd_a_log)), dexp=_rows(jnp.repeat(ssd_d, SSD_HEAD_DIM, axis=1)),
        normg=_rows(ssd_norm), wout=ssd_w_out.astype(BF16))
    ffn = dict(
        gpre=_rows(norm_ffn_pre), gpost=_rows(norm_ffn_post),
        wg=ffn_w_gate.astype(BF16), wu=ffn_w_up.astype(BF16), wd=ffn_w_down.astype(BF16))

    bp, seq, _ = x_prompt.shape
    x = x_prompt
    p_lc, p_lh, p_sc, p_ss = [], [], [], []
    for layer in range(DEPTH):
        j = layer // 2
        if layer % 2 == 0:
            x, conv, h = _lru_prompt_call(x, lru, j, ffn, layer)
            p_lc.append(conv[:, CARRY_LO:])
            p_lh.append(h[:, 0])
        else:
            x, conv, st = _ssd_prompt_call(x, ssd, j, ffn, layer)
            p_sc.append(conv[:, CARRY_LO:])
            p_ss.append(st.reshape(bp, SSD_HEADS, SSD_HEAD_DIM, SSD_STATE))
    y_prompt = x

    bs, sl, _ = x_sample.shape
    n = bs * sl
    xs = x_sample.reshape(n, D_MODEL)
    s_lc, s_lh, s_sc = [], [], []
    s_ss_all = None
    for layer in range(DEPTH):
        j = layer // 2
        if layer % 2 == 0:
            g, rec = _dense_call(_lru_pre_kernel, "lru_pre", (xs,), _pick(lru, j, ('gpre', 'win')),
                                 (D_RNN, D_RNN), [])
            gated, conv, h = _lru_core_sample_call(
                g.reshape(bs, sl, D_RNN), rec.reshape(bs, sl, D_RNN), state_lru_conv,
                state_lru_h.reshape(n_a, bs, 1, D_RNN), j, lru)
            s_lc.append(conv[:, CARRY_LO:])
            s_lh.append(h[:, 0])
            (xs,) = _dense_call(
                _lru_post_ffn_kernel, "lru_post_ffn", (xs, gated.reshape(n, D_RNN)),
                _pick(lru, j, ('wout', 'gpost')) + _pick(ffn, layer, _FFN),
                (D_MODEL,), [pltpu.VMEM((n, D_FF), BF16)])
        else:
            z, xbc, dt = _dense_call(_ssd_pre_kernel, "ssd_pre", (xs,),
                                     _pick(ssd, j, ('gpre', 'wzx', 'wdt', 'dtb')),
                                     (D_INNER, SSD_CONV_DIM, LANES), [])
            y, conv, s_ss_all = _ssd_core_sample_call(
                xbc.reshape(bs, sl, SSD_CONV_DIM), dt.reshape(bs, sl, LANES), state_ssd_conv,
                state_ssd.reshape(n_b, bs, D_INNER, SSD_STATE), j, s_ss_all, ssd)
            s_sc.append(conv[:, CARRY_LO:])
            (xs,) = _dense_call(
                _ssd_post_ffn_kernel, "ssd_post_ffn", (xs, y.reshape(n, D_INNER), z),
                _pick(ssd, j, ('normg', 'wout', 'gpost')) + _pick(ffn, layer, _FFN),
                (D_MODEL,), [pltpu.VMEM((n, D_INNER), BF16), pltpu.VMEM((n, D_FF), BF16)])
    y_sample = xs.reshape(bs, sl, D_MODEL)

    return (y_prompt, y_sample,
            jnp.stack(p_lc), jnp.stack(p_lh), jnp.stack(p_sc), jnp.stack(p_ss),
            jnp.stack(s_lc), jnp.stack(s_lh), jnp.stack(s_sc),
            s_ss_all.reshape(n_b, bs, SSD_HEADS, SSD_HEAD_DIM, SSD_STATE))
```

```python
import functools
import math

import jax
import jax.numpy as jnp
from jax import lax
from jax.experimental import pallas as pl
from jax.experimental.pallas import tpu as pltpu

F32 = jnp.float32
BF16 = jnp.bfloat16

D_MODEL = 1024
DEPTH = 4
CONV_WIDTH = 4
EPS = 1e-6
D_RNN = D_MODEL
LRU_BLOCKS = 4
LRU_BLOCK_W = D_RNN // LRU_BLOCKS
LRU_C = 8.0
D_INNER = 2 * D_MODEL
SSD_HEAD_DIM = 64
SSD_HEADS = D_INNER // SSD_HEAD_DIM
SSD_GROUPS = 4
SSD_HPG = SSD_HEADS // SSD_GROUPS
SSD_STATE = 128
SSD_GN = SSD_GROUPS * SSD_STATE
SSD_CONV_DIM = D_INNER + 2 * SSD_GN
D_FF = -(-8 * D_MODEL // (3 * 256)) * 256

LANES = 128
SUBLANES = 8
CARRY_ROWS = SUBLANES
CARRY_LO = CARRY_ROWS - (CONV_WIDTH - 1)
SSD_Q = 128
GROUP_W = D_INNER // SSD_GROUPS
PAIR_W = 2 * SSD_HEAD_DIM
FF_BLK = 256
PROJ_BLK = 512
LRU_T = 512
SSD_T = 256
VMEM_LIMIT = 60000 * 1024


def _rmsnorm(x, g):
    ms = jnp.mean(x * x, axis=-1, keepdims=True)
    return x * lax.rsqrt(ms + EPS) * g


def _softplus(x):
    return jnp.maximum(x, 0.0) + jnp.log1p(jnp.exp(-jnp.abs(x)))


def _silu(x):
    return x * jax.nn.sigmoid(x)


def _gelu_tanh(x):
    c = math.sqrt(2.0 / math.pi)
    return x * (0.5 * (1.0 + jnp.tanh(c * (x + 0.044715 * (x * x * x)))))


def _dot(a, b):
    return jnp.dot(a, b, preferred_element_type=F32)


def _conv_taps(full_ref, rows, w, b, col):
    out = b + full_ref[CARRY_LO:CARRY_LO + rows, col] * w[0:1]
    for k in range(1, CONV_WIDTH):
        out = out + full_ref[CARRY_LO + k:CARRY_LO + k + rows, col] * w[k:k + 1]
    return out


def _run(steps):
    try:
        while True:
            next(steps)
    except StopIteration as done:
        return done.value


def _weave(main, side, lead):
    side_result = None
    side_done = False
    try:
        want = lead
        while True:
            for _ in range(want):
                if not side_done:
                    try:
                        next(side)
                    except StopIteration as done:
                        side_result, side_done = done.value, True
            want = next(main)
    except StopIteration as done:
        main_result = done.value
    if not side_done:
        side_result = _run(side)
    return main_result, side_result


def _ffn_steps(load_x, gpre, gpost, wg_ref, wu_ref, wd_ref, act_ref):
    hn = _rmsnorm(load_x(), gpre).astype(BF16)
    yield
    for c in range(D_FF // FF_BLK):
        col = slice(c * FF_BLK, (c + 1) * FF_BLK)
        gate = _dot(hn, wg_ref[:, col])
        up = _dot(hn, wu_ref[:, col])
        act_ref[:, col] = (_silu(gate) * up).astype(BF16)
        yield
    parts = []
    for c in range(D_MODEL // FF_BLK):
        parts.append(_dot(act_ref[...], wd_ref[:, c * FF_BLK:(c + 1) * FF_BLK]))
        yield
    f = jnp.concatenate(parts, axis=1)
    return load_x() + _rmsnorm(f, gpost)


def _ffn(x, gpre, gpost, wg_ref, wu_ref, wd_ref, act_ref):
    return _run(_ffn_steps(lambda: x, gpre, gpost, wg_ref, wu_ref, wd_ref, act_ref))


def _lru_pre(x, gpre, win_ref):
    hn = _rmsnorm(x, gpre).astype(BF16)
    gate = _dot(hn, win_ref[:, :D_RNN])
    rec = _dot(hn, win_ref[:, D_RNN:])
    return _gelu_tanh(gate), rec


def _scan8(a, u):
    rows, ch = a.shape
    a3 = a.reshape(rows // SUBLANES, SUBLANES, ch)
    u3 = u.reshape(rows // SUBLANES, SUBLANES, ch)
    sub = lax.broadcasted_iota(jnp.int32, a3.shape, 1)
    for s in (1, 2, 4):
        m = sub >= s
        a_sh = jnp.where(m, pltpu.roll(a3, s, 1), 1.0)
        u_sh = jnp.where(m, pltpu.roll(u3, s, 1), 0.0)
        u3 = a3 * u_sh + u3
        a3 = a3 * a_sh
    return a3.reshape(rows, ch), u3.reshape(rows, ch)


def _lru_core_steps(rows, full_ref, g_ref, a_ref, u_ref, h_b, cw, cb, wr_ref, br, wi_ref, bi, lam):
    sp = _softplus(-lam)
    for k in range(LRU_BLOCKS):
        yield
        col = slice(k * LRU_BLOCK_W, (k + 1) * LRU_BLOCK_W)
        xc = _conv_taps(full_ref, rows, cw[:, col], cb[:, col], col)
        xcb = xc.astype(BF16)
        r = jax.nn.sigmoid(_dot(xcb, wr_ref[k]) + br[:, col])
        i = jax.nn.sigmoid(_dot(xcb, wi_ref[k]) + bi[:, col])
        log_a = (-LRU_C * r) * sp[:, col]
        a = jnp.exp(log_a)
        u = jnp.sqrt(-jnp.tanh(log_a) * (a * a + 1.0)) * (i * xc)
        a8, u8 = _scan8(a, u)
        a_ref[:, col] = a8
        u_ref[:, col] = u8

    yield
    for j in range(rows // SUBLANES):
        blk = slice(j * SUBLANES, (j + 1) * SUBLANES)
        hs = u_ref[blk, :] + a_ref[blk, :] * h_b
        u_ref[blk, :] = hs
        h_b = jnp.broadcast_to(hs[SUBLANES - 1:SUBLANES, :], hs.shape)
    return u_ref[...] * g_ref[...], h_b


def _lru_core(*args):
    return _run(_lru_core_steps(*args))


def _lru_post(x, gated, wout_ref, gpost):
    mix = _dot(gated.astype(BF16), wout_ref[...])
    return x + _rmsnorm(mix, gpost)


def _tile_flags(tiles_per_seq):
    step = pl.program_id(0)
    in_seq = lax.rem(step, tiles_per_seq)
    drain = step == pl.num_programs(0) - 1
    return (in_seq == 0) & jnp.logical_not(drain), in_seq == tiles_per_seq - 1, step == 0


def _lru_layer_prompt_kernel(tiles_per_seq, x_ref, gpre_ref, gpost_ref, win_ref, cw_ref, cb_ref, wr_ref, br_ref,
                             wi_ref, bi_ref, lam_ref, wout_ref,
                             fpre_ref, fpost_ref, wg_ref, wu_ref, wd_ref,
                             o_ref, conv_o_ref, h_o_ref,
                             full_ref, g_ref, a_ref, u_ref, h_ref, xmid_ref, act_ref):
    rows = x_ref.shape[0]
    first_tile, last_tile, first_step = _tile_flags(tiles_per_seq)

    @pl.when(first_tile)
    def _():
        full_ref[0:CARRY_ROWS, :] = jnp.zeros((CARRY_ROWS, D_RNN), F32)
        h_ref[...] = jnp.zeros(h_ref.shape, F32)

    @pl.when(first_step)
    def _():
        xmid_ref[...] = jnp.zeros(xmid_ref.shape, F32)

    def mixer():
        x = x_ref[...]
        hn = _rmsnorm(x, gpre_ref[...]).astype(BF16)
        g_ref[...] = _gelu_tanh(_dot(hn, win_ref[:, :D_RNN]))
        yield 1
        full_ref[CARRY_ROWS:CARRY_ROWS + rows, :] = _dot(hn, win_ref[:, D_RNN:])
        core = _lru_core_steps(rows, full_ref, g_ref, a_ref, u_ref, h_ref[...], cw_ref[...], cb_ref[...],
                               wr_ref, br_ref[...], wi_ref, bi_ref[...], lam_ref[...])
        pieces = 0
        try:
            while True:
                next(core)
                pieces += 1
                yield 3 if pieces <= LRU_BLOCKS else 1
        except StopIteration as done:
            gated, h_b = done.value
        h_ref[...] = h_b
        tail = full_ref[rows:rows + CARRY_ROWS, :]
        full_ref[0:CARRY_ROWS, :] = tail
        yield 1
        return _lru_post(x, gated, wout_ref, gpost_ref[...]), tail, h_b

    ffn = _ffn_steps(lambda: xmid_ref[...], fpre_ref[...], fpost_ref[...], wg_ref, wu_ref, wd_ref, act_ref)
    (x_new, tail, h_b), o_ref[...] = _weave(mixer(), ffn, lead=2)
    xmid_ref[...] = x_new

    @pl.when(last_tile)
    def _():
        conv_o_ref[...] = tail
        h_o_ref[...] = h_b[0:1, :]


def _lru_pre_kernel(x_ref, gpre_ref, win_ref, g_o_ref, rec_o_ref):
    g, rec = _lru_pre(x_ref[...], gpre_ref[...], win_ref)
    g_o_ref[...] = g
    rec_o_ref[...] = rec


def _lru_core_sample_kernel(g_ref, rec_ref, conv_ref, h_ref, cw_ref, cb_ref, wr_ref, br_ref,
                            wi_ref, bi_ref, lam_ref,
                            gated_o_ref, conv_o_ref, h_o_ref,
                            full_ref, a_ref, u_ref):
    rows = rec_ref.shape[0]
    full_ref[0:CARRY_ROWS, :] = jnp.zeros((CARRY_ROWS, D_RNN), F32)
    full_ref[CARRY_LO:CARRY_ROWS, :] = conv_ref[...]
    full_ref[CARRY_ROWS:CARRY_ROWS + rows, :] = rec_ref[...]
    h_b = jnp.broadcast_to(h_ref[...], (SUBLANES, D_RNN))
    gated, h_b = _lru_core(rows, full_ref, g_ref, a_ref, u_ref, h_b, cw_ref[...], cb_ref[...],
                           wr_ref, br_ref[...], wi_ref, bi_ref[...], lam_ref[...])
    gated_o_ref[...] = gated
    conv_o_ref[...] = full_ref[rows:rows + CARRY_ROWS, :]
    h_o_ref[...] = h_b[0:1, :]


def _lru_post_ffn_kernel(x_ref, gated_ref, wout_ref, gpost_ref, fpre_ref, fpost_ref,
                         wg_ref, wu_ref, wd_ref, o_ref, act_ref):
    x = _lru_post(x_ref[...], gated_ref[...], wout_ref, gpost_ref[...])
    o_ref[...] = _ffn(x, fpre_ref[...], fpost_ref[...], wg_ref, wu_ref, wd_ref, act_ref)


def _ssd_pre(x, gpre, wzx_ref, wdt_ref, dtb, z_ref, xbc_ref, xbc_row0):
    rows = x.shape[0]
    hn = _rmsnorm(x, gpre).astype(BF16)
    for c in range(D_INNER // PROJ_BLK):
        col = slice(c * PROJ_BLK, (c + 1) * PROJ_BLK)
        z_ref[:, col] = _dot(hn, wzx_ref[:, col])
    for c in range(SSD_CONV_DIM // PROJ_BLK):
        col = slice(c * PROJ_BLK, (c + 1) * PROJ_BLK)
        wcol = slice(D_INNER + c * PROJ_BLK, D_INNER + (c + 1) * PROJ_BLK)
        xbc_ref[xbc_row0:xbc_row0 + rows, col] = _dot(hn, wzx_ref[:, wcol])
    return _softplus(_dot(hn, wdt_ref[...]) + dtb)


def _ssd_conv_silu(full_ref, rows, cw, cb):
    for c in range(SSD_CONV_DIM // PROJ_BLK):
        col = slice(c * PROJ_BLK, (c + 1) * PROJ_BLK)
        out = _conv_taps(full_ref, rows, cw[:, col], cb[:, col], col)
        full_ref[CARRY_ROWS:CARRY_ROWS + rows, col] = _silu(out)


def _cumsum_rows(x):
    rows = x.shape[0]
    row = lax.broadcasted_iota(jnp.int32, x.shape, 0)
    s = 1
    while s < rows:
        x = x + jnp.where(row >= s, pltpu.roll(x, s, 0), 0.0)
        s *= 2
    return x


def _ssd_core_steps(rows, live, full_ref, dt_ref, a_neg, d_exp, st_ref, y_ref, native_state=False):
    q = SSD_Q
    v = live
    tril = (lax.broadcasted_iota(jnp.int32, (v, q), 0) >= lax.broadcasted_iota(jnp.int32, (v, q), 1))
    left = lax.broadcasted_iota(jnp.int32, (v, PAIR_W), 1) < SSD_HEAD_DIM
    left_q = lax.broadcasted_iota(jnp.int32, (q, PAIR_W), 1) < SSD_HEAD_DIM
    contract_last = (((1,), (1,)), ((), ()))
    contract_first = (((0,), (0,)), ((), ()))

    def pad_rows(a):
        if v == q:
            return a
        return jnp.concatenate([a, jnp.zeros((q - v,) + a.shape[1:], a.dtype)], axis=0)

    for c in range(rows // q):
        r0 = c * q
        rf = c * q + CARRY_ROWS
        dtc = dt_ref[pl.ds(r0, v), :]
        acs = _cumsum_rows(dtc * a_neg)
        acs_t = pad_rows(acs).T
        for g in range(SSD_GROUPS):
            yield
            gcol = slice(g * GROUP_W, (g + 1) * GROUP_W)
            bcol = slice(D_INNER + g * SSD_STATE, D_INNER + (g + 1) * SSD_STATE)
            ccol = slice(D_INNER + SSD_GN + g * SSD_STATE, D_INNER + SSD_GN + (g + 1) * SSD_STATE)
            b_g = full_ref[pl.ds(rf, q), bcol].astype(BF16)
            c_g = full_ref[pl.ds(rf, v), ccol].astype(BF16)
            cb = lax.dot_general(c_g, b_g, contract_last, preferred_element_type=F32)
            if native_state:
                y_off = lax.dot_general(c_g, st_ref[gcol, :].astype(BF16), contract_last,
                                        preferred_element_type=F32)
            else:
                y_off = _dot(c_g, st_ref[:, gcol].astype(BF16))
            xdd = []
            dec = []
            for pr in range(SSD_HPG // 2):
                h1 = g * SSD_HPG + 2 * pr
                h2 = h1 + 1
                pcol = slice(h1 * SSD_HEAD_DIM, h1 * SSD_HEAD_DIM + PAIR_W)
                a1 = jnp.broadcast_to(acs[:, h1:h1 + 1], (v, q))
                a2 = jnp.broadcast_to(acs[:, h2:h2 + 1], (v, q))
                l1 = jnp.exp(jnp.where(tril, a1 - acs_t[h1:h1 + 1, :], -jnp.inf))
                l2 = jnp.exp(jnp.where(tril, a2 - acs_t[h2:h2 + 1, :], -jnp.inf))
                m = jnp.concatenate([cb * l1, cb * l2], axis=1).astype(BF16)
                a_pair = jnp.where(left, a1, a2)
                dt_pair = jnp.where(left, jnp.broadcast_to(dtc[:, h1:h1 + 1], (v, PAIR_W)),
                                    jnp.broadcast_to(dtc[:, h2:h2 + 1], (v, PAIR_W)))
                xs = full_ref[pl.ds(rf, v), pcol]
                xdt = xs * dt_pair
                xdt_q = pad_rows(xdt)
                rhs = jnp.concatenate([jnp.where(left_q, xdt_q, 0.0), jnp.where(left_q, 0.0, xdt_q)],
                                      axis=0).astype(BF16)
                y_diag = _dot(m, rhs)
                a_last = a_pair[v - 1:v, :]
                xdd.append(pad_rows((xdt * jnp.exp(a_last - a_pair)).astype(BF16)))
                if native_state:
                    e_last = jnp.exp(a_last)
                    for lane in (0, SSD_HEAD_DIM):
                        dec.append(jnp.broadcast_to(e_last[:, lane:lane + 1], (SSD_HEAD_DIM, SSD_STATE)))
                else:
                    dec.append(jnp.exp(a_last))
                ocol = slice(pr * PAIR_W, (pr + 1) * PAIR_W)
                y_ref[pl.ds(r0, v), pcol] = (y_diag + y_off[:, ocol] * jnp.exp(a_pair)
                                             + d_exp[:, pcol] * xs)
            if native_state:
                states = lax.dot_general(jnp.concatenate(xdd, axis=1), b_g, contract_first,
                                         preferred_element_type=F32)
                st_ref[gcol, :] = st_ref[gcol, :] * jnp.concatenate(dec, axis=0) + states
            else:
                states = lax.dot_general(b_g, jnp.concatenate(xdd, axis=1), contract_first,
                                         preferred_element_type=F32)
                st_ref[:, gcol] = st_ref[:, gcol] * jnp.concatenate(dec, axis=1) + states


def _ssd_core(*args):
    return _run(_ssd_core_steps(*args))


def _ssd_post_steps(x, y_ref, z_ref, normg, wout_ref, gpost, ybf_ref):
    for g in range(SSD_GROUPS):
        yield
        col = slice(g * GROUP_W, (g + 1) * GROUP_W)
        yg = y_ref[:, col] * _silu(z_ref[:, col])
        ms = jnp.mean(yg * yg, axis=-1, keepdims=True)
        ybf_ref[:, col] = (yg * lax.rsqrt(ms + EPS) * normg[:, col]).astype(BF16)
    mix = _dot(ybf_ref[...], wout_ref[...])
    return x + _rmsnorm(mix, gpost)


def _ssd_post(*args):
    return _run(_ssd_post_steps(*args))


def _ssd_layer_prompt_kernel(tiles_per_seq, x_ref, gpre_ref, gpost_ref, wzx_ref, wdt_ref, dtb_ref, cw_ref, cb_ref,
                             alog_ref, dexp_ref, normg_ref, wout_ref,
                             fpre_ref, fpost_ref, wg_ref, wu_ref, wd_ref,
                             o_ref, conv_o_ref, st_o_ref,
                             full_ref, z_ref, dt_ref, y_ref, ybf_ref, st_ref, xmid_ref, act_ref):
    rows = x_ref.shape[0]
    first_tile, last_tile, first_step = _tile_flags(tiles_per_seq)

    @pl.when(first_tile)
    def _():
        full_ref[0:CARRY_ROWS, :] = jnp.zeros((CARRY_ROWS, SSD_CONV_DIM), F32)
        st_ref[...] = jnp.zeros(st_ref.shape, F32)

    @pl.when(first_step)
    def _():
        xmid_ref[...] = jnp.zeros(xmid_ref.shape, F32)

    def mixer():
        x = x_ref[...]
        hn = _rmsnorm(x, gpre_ref[...]).astype(BF16)
        dt_ref[...] = _softplus(_dot(hn, wdt_ref[...]) + dtb_ref[...])
        cw = cw_ref[...]
        cb = cb_ref[...]
        n_blk = SSD_CONV_DIM // PROJ_BLK

        def project(c):
            wcol = slice(D_INNER + c * PROJ_BLK, D_INNER + (c + 1) * PROJ_BLK)
            full_ref[CARRY_ROWS:CARRY_ROWS + rows, c * PROJ_BLK:(c + 1) * PROJ_BLK] = _dot(hn, wzx_ref[:, wcol])

        project(0)
        for c in range(n_blk):
            if c + 1 < n_blk:
                project(c + 1)
            col = slice(c * PROJ_BLK, (c + 1) * PROJ_BLK)
            tail_c = full_ref[rows:rows + CARRY_ROWS, col]
            out = _conv_taps(full_ref, rows, cw[:, col], cb[:, col], col)
            full_ref[CARRY_ROWS:CARRY_ROWS + rows, col] = _silu(out)
            full_ref[0:CARRY_ROWS, col] = tail_c
            yield 1

        z_blocks = iter(range(D_INNER // PROJ_BLK))
        core = _ssd_core_steps(rows, SSD_Q, full_ref, dt_ref, -jnp.exp(alog_ref[...]), dexp_ref[...],
                               st_ref, y_ref)
        for _ in core:
            c = next(z_blocks, None)
            if c is None:
                yield 1
            else:
                col = slice(c * PROJ_BLK, (c + 1) * PROJ_BLK)
                z_ref[:, col] = _dot(hn, wzx_ref[:, col])
                yield c % 2
        post = _ssd_post_steps(x, y_ref, z_ref, normg_ref[...], wout_ref, gpost_ref[...], ybf_ref)
        try:
            while True:
                next(post)
                yield 1
        except StopIteration as done:
            return done.value

    ffn = _ffn_steps(lambda: xmid_ref[...], fpre_ref[...], fpost_ref[...], wg_ref, wu_ref, wd_ref, act_ref)
    x_new, o_ref[...] = _weave(mixer(), ffn, lead=2)
    xmid_ref[...] = x_new

    @pl.when(last_tile)
    def _():
        conv_o_ref[...] = full_ref[0:CARRY_ROWS, :]
        st_o_ref[...] = st_ref[...].T


def _ssd_pre_kernel(x_ref, gpre_ref, wzx_ref, wdt_ref, dtb_ref, z_o_ref, xbc_o_ref, dt_o_ref):
    dt_o_ref[...] = _ssd_pre(x_ref[...], gpre_ref[...], wzx_ref, wdt_ref, dtb_ref[...],
                             z_o_ref, xbc_o_ref, 0)


def _ssd_core_sample_body(xbc_ref, dt_in_ref, conv_ref, st_in_ref, cw_ref, cb_ref, alog_ref, dexp_ref,
                          y_o_ref, conv_o_ref, st_o_ref,
                          full_ref):
    rows = xbc_ref.shape[0]
    full_ref[...] = jnp.zeros(full_ref.shape, F32)
    full_ref[CARRY_LO:CARRY_ROWS, :] = conv_ref[...]
    full_ref[CARRY_ROWS:CARRY_ROWS + rows, :] = xbc_ref[...]
    conv_o_ref[...] = full_ref[rows:rows + CARRY_ROWS, :]
    _ssd_conv_silu(full_ref, rows, cw_ref[...], cb_ref[...])
    st_o_ref[...] = st_in_ref[...]
    _ssd_core(SSD_Q, rows, full_ref, dt_in_ref, -jnp.exp(alog_ref[...]), dexp_ref[...], st_o_ref, y_o_ref,
              True)


_SSD_CORE_SAMPLE_INS = 8


def _ssd_core_sample_kernel(j, aliased, *refs):
    if aliased:
        _ssd_core_sample_body(*refs[:_SSD_CORE_SAMPLE_INS], *refs[_SSD_CORE_SAMPLE_INS + 1:])
        return
    slot = pl.program_id(0)

    @pl.when(slot == j)
    def _():
        _ssd_core_sample_body(*refs)

    @pl.when(slot != j)
    def _():
        st_o_ref = refs[_SSD_CORE_SAMPLE_INS + 2]
        st_o_ref[...] = jnp.zeros(st_o_ref.shape, F32)


def _ssd_post_ffn_kernel(x_ref, y_ref, z_ref, normg_ref, wout_ref, gpost_ref, fpre_ref, fpost_ref,
                         wg_ref, wu_ref, wd_ref, o_ref, ybf_ref, act_ref):
    x = _ssd_post(x_ref[...], y_ref, z_ref, normg_ref[...], wout_ref, gpost_ref[...], ybf_ref)
    o_ref[...] = _ffn(x, fpre_ref[...], fpost_ref[...], wg_ref, wu_ref, wd_ref, act_ref)


def _layer_block(stacked, idx):
    nd = stacked.ndim
    return pl.BlockSpec((None,) + stacked.shape[1:], lambda *_: (idx,) + (0,) * (nd - 1),
                        pipeline_mode=pl.Buffered(1))


def _pick(params, idx, names):
    return [(params[n], idx) for n in names]


_LRU_MIX = ('gpre', 'gpost', 'win', 'cw', 'cb', 'wr', 'br', 'wi', 'bi', 'lam', 'wout')
_LRU_CORE = ('cw', 'cb', 'wr', 'br', 'wi', 'bi', 'lam')
_SSD_MIX = ('gpre', 'gpost', 'wzx', 'wdt', 'dtb', 'cw', 'cb', 'alog', 'dexp', 'normg', 'wout')
_SSD_CORE = ('cw', 'cb', 'alog', 'dexp')
_FFN = ('gpre', 'gpost', 'wg', 'wu', 'wd')


def _params(n_axes):
    return pltpu.CompilerParams(dimension_semantics=("arbitrary",) * n_axes,
                                vmem_limit_bytes=VMEM_LIMIT)


def _prompt_layer_call(kernel, name, x, t, consts, state_shapes, scratch):
    bsz, seq, _ = x.shape
    nt = seq // t
    n_tiles = bsz * nt

    def mixer_tile(i):
        g = jnp.minimum(i, n_tiles - 1)
        return g // nt, g % nt

    def ffn_tile(i):
        g = jnp.maximum(i - 1, 0)
        return g // nt, g % nt

    return pl.pallas_call(
        functools.partial(kernel, nt),
        grid=(n_tiles + 1,),
        in_specs=[pl.BlockSpec((None, t, D_MODEL), lambda i: (*mixer_tile(i), 0))]
        + [_layer_block(a, k) for a, k in consts],
        out_specs=[pl.BlockSpec((None, t, D_MODEL), lambda i: (*ffn_tile(i), 0))]
        + [pl.BlockSpec((None,) + s, lambda i: (mixer_tile(i)[0], 0, 0)) for s in state_shapes],
        out_shape=[jax.ShapeDtypeStruct((bsz, seq, D_MODEL), F32)]
        + [jax.ShapeDtypeStruct((bsz,) + s, F32) for s in state_shapes],
        scratch_shapes=scratch + [pltpu.VMEM((t, D_MODEL), F32), pltpu.VMEM((t, D_FF), BF16)],
        compiler_params=_params(1),
        name=name,
    )(x, *[a for a, _ in consts])


def _lru_prompt_call(x, lru, j, ffn, layer):
    t = LRU_T
    consts = _pick(lru, j, _LRU_MIX) + _pick(ffn, layer, _FFN)
    return _prompt_layer_call(
        _lru_layer_prompt_kernel, "lru_layer_prompt", x, t, consts,
        [(CARRY_ROWS, D_RNN), (1, D_RNN)],
        [pltpu.VMEM((t + CARRY_ROWS, D_RNN), F32),
         pltpu.VMEM((t, D_RNN), F32),
         pltpu.VMEM((t, D_RNN), F32),
         pltpu.VMEM((t, D_RNN), F32),
         pltpu.VMEM((SUBLANES, D_RNN), F32)])


def _ssd_prompt_call(x, ssd, j, ffn, layer):
    t = SSD_T
    consts = _pick(ssd, j, _SSD_MIX) + _pick(ffn, layer, _FFN)
    return _prompt_layer_call(
        _ssd_layer_prompt_kernel, "ssd_layer_prompt", x, t, consts,
        [(CARRY_ROWS, SSD_CONV_DIM), (D_INNER, SSD_STATE)],
        [pltpu.VMEM((t + CARRY_ROWS, SSD_CONV_DIM), F32),
         pltpu.VMEM((t, D_INNER), F32),
         pltpu.VMEM((t, LANES), F32),
         pltpu.VMEM((t, D_INNER), F32),
         pltpu.VMEM((t, D_INNER), BF16),
         pltpu.VMEM((SSD_STATE, D_INNER), F32)])


def _dense_call(kernel, name, row_ins, consts, out_widths, scratch):
    n = row_ins[0].shape[0]

    def rows_block(width):
        return pl.BlockSpec((n, width), lambda i: (0, 0))

    return pl.pallas_call(
        kernel,
        grid=(1,),
        in_specs=[rows_block(a.shape[1]) for a in row_ins] + [_layer_block(a, k) for a, k in consts],
        out_specs=[rows_block(w) for w in out_widths],
        out_shape=[jax.ShapeDtypeStruct((n, w), F32) for w in out_widths],
        scratch_shapes=scratch,
        compiler_params=_params(1),
        name=name,
    )(*row_ins, *[a for a, _ in consts])


def _per_seq(rows, width):
    return pl.BlockSpec((None, rows, width), lambda b: (b, 0, 0))


def _per_layer_seq(j, rows, width):
    return pl.BlockSpec((None, None, rows, width), lambda b: (j, b, 0, 0))


def _lru_core_sample_call(g, rec, conv_all, h_all, j, lru):
    bsz, seq, _ = g.shape
    consts = _pick(lru, j, _LRU_CORE)
    per_seq = _per_seq
    return pl.pallas_call(
        _lru_core_sample_kernel,
        grid=(bsz,),
        in_specs=[per_seq(seq, D_RNN), per_seq(seq, D_RNN), _per_layer_seq(j, CONV_WIDTH - 1, D_RNN),
                  _per_layer_seq(j, 1, D_RNN)] + [_layer_block(a, k) for a, k in consts],
        out_specs=[per_seq(seq, D_RNN), per_seq(CARRY_ROWS, D_RNN), per_seq(1, D_RNN)],
        out_shape=[jax.ShapeDtypeStruct((bsz, seq, D_RNN), F32),
                   jax.ShapeDtypeStruct((bsz, CARRY_ROWS, D_RNN), F32),
                   jax.ShapeDtypeStruct((bsz, 1, D_RNN), F32)],
        scratch_shapes=[pltpu.VMEM((seq + CARRY_ROWS, D_RNN), F32),
                        pltpu.VMEM((seq, D_RNN), F32),
                        pltpu.VMEM((seq, D_RNN), F32)],
        compiler_params=_params(1),
        name="lru_core_sample",
    )(g, rec, conv_all, h_all, *[a for a, _ in consts])


def _ssd_core_sample_call(xbc, dt, conv_all, st_all, j, st_new_all, ssd):
    bsz, seq, _ = xbc.shape
    consts = _pick(ssd, j, _SSD_CORE)
    aliased = st_new_all is not None
    n_slots = 1 if aliased else st_all.shape[0]

    def seq_of(slot, b):
        if aliased:
            return b
        return jnp.where(slot == j, b, jnp.where(slot < j, 0, bsz - 1))

    def per_seq(rows, width):
        return pl.BlockSpec((None, rows, width), lambda slot, b: (seq_of(slot, b), 0, 0))

    def per_layer_seq(rows, width):
        return pl.BlockSpec((None, None, rows, width), lambda slot, b: (j, seq_of(slot, b), 0, 0))

    out_state = pl.BlockSpec((None, None, D_INNER, SSD_STATE),
                             lambda slot, b: (j if aliased else slot, b, 0, 0))
    in_specs = [per_seq(seq, SSD_CONV_DIM), per_seq(seq, LANES), per_layer_seq(CONV_WIDTH - 1, SSD_CONV_DIM),
                per_layer_seq(D_INNER, SSD_STATE)] + [_layer_block(a, k) for a, k in consts]
    args = [xbc, dt, conv_all, st_all, *[a for a, _ in consts]]
    assert len(args) == _SSD_CORE_SAMPLE_INS
    aliases = {}
    if aliased:
        in_specs.append(pl.BlockSpec(memory_space=pl.ANY))
        aliases = {len(args): 2}
        args.append(st_new_all)
    return pl.pallas_call(
        functools.partial(_ssd_core_sample_kernel, j, aliased),
        grid=(n_slots, bsz),
        in_specs=in_specs,
        out_specs=[per_seq(seq, D_INNER), per_seq(CARRY_ROWS, SSD_CONV_DIM), out_state],
        out_shape=[jax.ShapeDtypeStruct((bsz, seq, D_INNER), F32),
                   jax.ShapeDtypeStruct((bsz, CARRY_ROWS, SSD_CONV_DIM), F32),
                   jax.ShapeDtypeStruct(st_all.shape, F32)],
        scratch_shapes=[pltpu.VMEM((SSD_Q + CARRY_ROWS, SSD_CONV_DIM), F32)],
        input_output_aliases=aliases,
        compiler_params=_params(2),
        name="ssd_core_sample",
    )(*args)


def _rows(v):
    return v.reshape(v.shape[0], 1, -1).astype(F32)


def _pad_lanes(v):
    return jnp.pad(v, ((0, 0),) * (v.ndim - 1) + ((0, LANES - v.shape[-1]),))


def kernel(x_prompt, x_sample, state_lru_conv, state_lru_h, state_ssd_conv, state_ssd, norm_mix_pre, norm_mix_post, norm_ffn_pre, norm_ffn_post, lru_w_in, lru_conv_w, lru_conv_b, lru_w_r, lru_b_r, lru_w_i, lru_b_i, lru_lambda, lru_w_out, ssd_w_in, ssd_conv_w, ssd_conv_b, ssd_dt_bias, ssd_a_log, ssd_d, ssd_norm, ssd_w_out, ffn_w_gate, ffn_w_up, ffn_w_down):
    n_a = lru_w_in.shape[0]
    n_b = ssd_w_in.shape[0]
    lru = dict(
        gpre=_rows(norm_mix_pre[0::2]), gpost=_rows(norm_mix_post[0::2]),
        win=lru_w_in.astype(BF16), cw=lru_conv_w, cb=_rows(lru_conv_b),
        wr=lru_w_r.astype(BF16), br=_rows(lru_b_r), wi=lru_w_i.astype(BF16), bi=_rows(lru_b_i),
        lam=_rows(lru_lambda), wout=lru_w_out.astype(BF16))
    ssd = dict(
        gpre=_rows(norm_mix_pre[1::2]), gpost=_rows(norm_mix_post[1::2]),
        wzx=ssd_w_in.astype(BF16),
        wdt=_pad_lanes(ssd_w_in[:, :, D_INNER + SSD_CONV_DIM:]).astype(BF16),
        dtb=_pad_lanes(_rows(ssd_dt_bias)), cw=ssd_conv_w, cb=_rows(ssd_conv_b),
        alog=_pad_lanes(_rows(ssd_a_log)), dexp=_rows(jnp.repeat(ssd_d, SSD_HEAD_DIM, axis=1)),
        normg=_rows(ssd_norm), wout=ssd_w_out.astype(BF16))
    ffn = dict(
        gpre=_rows(norm_ffn_pre), gpost=_rows(norm_ffn_post),
        wg=ffn_w_gate.astype(BF16), wu=ffn_w_up.astype(BF16), wd=ffn_w_down.astype(BF16))

    bp, seq, _ = x_prompt.shape
    x = x_prompt
    p_lc, p_lh, p_sc, p_ss = [], [], [], []
    for layer in range(DEPTH):
        j = layer // 2
        if layer % 2 == 0:
            x, conv, h = _lru_prompt_call(x, lru, j, ffn, layer)
            p_lc.append(conv[:, CARRY_LO:])
            p_lh.append(h[:, 0])
        else:
            x, conv, st = _ssd_prompt_call(x, ssd, j, ffn, layer)
            p_sc.append(conv[:, CARRY_LO:])
            p_ss.append(st.reshape(bp, SSD_HEADS, SSD_HEAD_DIM, SSD_STATE))
    y_prompt = x

    bs, sl, _ = x_sample.shape
    n = bs * sl
    xs = x_sample.reshape(n, D_MODEL)
    s_lc, s_lh, s_sc = [], [], []
    s_ss_all = None
    for layer in range(DEPTH):
        j = layer // 2
        if layer % 2 == 0:
            g, rec = _dense_call(_lru_pre_kernel, "lru_pre", (xs,), _pick(lru, j, ('gpre', 'win')),
                                 (D_RNN, D_RNN), [])
            gated, conv, h = _lru_core_sample_call(
                g.reshape(bs, sl, D_RNN), rec.reshape(bs, sl, D_RNN), state_lru_conv,
                state_lru_h.reshape(n_a, bs, 1, D_RNN), j, lru)
            s_lc.append(conv[:, CARRY_LO:])
            s_lh.append(h[:, 0])
            (xs,) = _dense_call(
                _lru_post_ffn_kernel, "lru_post_ffn", (xs, gated.reshape(n, D_RNN)),
                _pick(lru, j, ('wout', 'gpost')) + _pick(ffn, layer, _FFN),
                (D_MODEL,), [pltpu.VMEM((n, D_FF), BF16)])
        else:
            z, xbc, dt = _dense_call(_ssd_pre_kernel, "ssd_pre", (xs,),
                                     _pick(ssd, j, ('gpre', 'wzx', 'wdt', 'dtb')),
                                     (D_INNER, SSD_CONV_DIM, LANES), [])
            y, conv, s_ss_all = _ssd_core_sample_call(
                xbc.reshape(bs, sl, SSD_CONV_DIM), dt.reshape(bs, sl, LANES), state_ssd_conv,
                state_ssd.reshape(n_b, bs, D_INNER, SSD_STATE), j, s_ss_all, ssd)
            s_sc.append(conv[:, CARRY_LO:])
            (xs,) = _dense_call(
                _ssd_post_ffn_kernel, "ssd_post_ffn", (xs, y.reshape(n, D_INNER), z),
                _pick(ssd, j, ('normg', 'wout', 'gpost')) + _pick(ffn, layer, _FFN),
                (D_MODEL,), [pltpu.VMEM((n, D_INNER), BF16), pltpu.VMEM((n, D_FF), BF16)])
    y_sample = xs.reshape(bs, sl, D_MODEL)

    return (y_prompt, y_sample,
            jnp.stack(p_lc), jnp.stack(p_lh), jnp.stack(p_sc), jnp.stack(p_ss),
            jnp.stack(s_lc), jnp.stack(s_lh), jnp.stack(s_sc),
            s_ss_all.reshape(n_b, bs, SSD_HEADS, SSD_HEAD_DIM, SSD_STATE))
```

```python
import functools
import math

import jax
import jax.numpy as jnp
from jax import lax
from jax.experimental import pallas as pl
from jax.experimental.pallas import tpu as pltpu

F32 = jnp.float32
BF16 = jnp.bfloat16

D_MODEL = 1024
DEPTH = 4
CONV_WIDTH = 4
EPS = 1e-6
D_RNN = D_MODEL
LRU_BLOCKS = 4
LRU_BLOCK_W = D_RNN // LRU_BLOCKS
LRU_C = 8.0
D_INNER = 2 * D_MODEL
SSD_HEAD_DIM = 64
SSD_HEADS = D_INNER // SSD_HEAD_DIM
SSD_GROUPS = 4
SSD_HPG = SSD_HEADS // SSD_GROUPS
SSD_STATE = 128
SSD_GN = SSD_GROUPS * SSD_STATE
SSD_CONV_DIM = D_INNER + 2 * SSD_GN
D_FF = -(-8 * D_MODEL // (3 * 256)) * 256

LANES = 128
SUBLANES = 8
CARRY_ROWS = SUBLANES
CARRY_LO = CARRY_ROWS - (CONV_WIDTH - 1)
SSD_Q = 128
SSD_IN_DIM = D_INNER + SSD_CONV_DIM + SSD_HEADS
DT_COLS = slice(D_INNER + SSD_CONV_DIM, D_INNER + SSD_CONV_DIM + LANES)
W_IN_PAD = DT_COLS.stop - SSD_IN_DIM
GROUP_W = D_INNER // SSD_GROUPS
PAIR_W = 2 * SSD_HEAD_DIM
FF_BLK = 256
PROJ_BLK = 512
LRU_T = 512
SSD_T = 256
VMEM_LIMIT = 60000 * 1024


def _rmsnorm(x, g):
    ms = jnp.mean(x * x, axis=-1, keepdims=True)
    return x * lax.rsqrt(ms + EPS) * g


def _softplus(x):
    return jnp.maximum(x, 0.0) + jnp.log1p(jnp.exp(-jnp.abs(x)))


def _silu(x):
    return x * jax.nn.sigmoid(x)


def _gelu_tanh(x):
    c = math.sqrt(2.0 / math.pi)
    return x * (0.5 * (1.0 + jnp.tanh(c * (x + 0.044715 * (x * x * x)))))


def _dot(a, b):
    return jnp.dot(a, b, preferred_element_type=F32)


def _conv_taps(full_ref, rows, w, b, col):
    out = b + full_ref[CARRY_LO:CARRY_LO + rows, col] * w[0:1]
    for k in range(1, CONV_WIDTH):
        out = out + full_ref[CARRY_LO + k:CARRY_LO + k + rows, col] * w[k:k + 1]
    return out


def _run(steps):
    try:
        while True:
            next(steps)
    except StopIteration as done:
        return done.value


def _weave(main, side, lead):
    side_result = None
    side_done = False
    try:
        want = lead
        while True:
            for _ in range(want):
                if not side_done:
                    try:
                        next(side)
                    except StopIteration as done:
                        side_result, side_done = done.value, True
            want = next(main)
    except StopIteration as done:
        main_result = done.value
    if not side_done:
        side_result = _run(side)
    return main_result, side_result


def _ffn_steps(load_x, gpre, gpost, wg_ref, wu_ref, wd_ref, act_ref):
    hn = _rmsnorm(load_x(), gpre).astype(BF16)
    yield
    for c in range(D_FF // FF_BLK):
        col = slice(c * FF_BLK, (c + 1) * FF_BLK)
        gate = _dot(hn, wg_ref[:, col])
        up = _dot(hn, wu_ref[:, col])
        act_ref[:, col] = (_silu(gate) * up).astype(BF16)
        yield
    parts = []
    for c in range(D_MODEL // FF_BLK):
        parts.append(_dot(act_ref[...], wd_ref[:, c * FF_BLK:(c + 1) * FF_BLK]))
        yield
    f = jnp.concatenate(parts, axis=1)
    return load_x() + _rmsnorm(f, gpost)


def _ffn(x, gpre, gpost, wg_ref, wu_ref, wd_ref, act_ref):
    return _run(_ffn_steps(lambda: x, gpre, gpost, wg_ref, wu_ref, wd_ref, act_ref))


def _lru_pre(x, gpre, win_ref):
    hn = _rmsnorm(x, gpre).astype(BF16)
    gate = _dot(hn, win_ref[:, :D_RNN])
    rec = _dot(hn, win_ref[:, D_RNN:])
    return _gelu_tanh(gate), rec


def _scan8(a, u):
    rows, ch = a.shape
    a3 = a.reshape(rows // SUBLANES, SUBLANES, ch)
    u3 = u.reshape(rows // SUBLANES, SUBLANES, ch)
    sub = lax.broadcasted_iota(jnp.int32, a3.shape, 1)
    for s in (1, 2, 4):
        m = sub >= s
        a_sh = jnp.where(m, pltpu.roll(a3, s, 1), 1.0)
        u_sh = jnp.where(m, pltpu.roll(u3, s, 1), 0.0)
        u3 = a3 * u_sh + u3
        a3 = a3 * a_sh
    return a3.reshape(rows, ch), u3.reshape(rows, ch)


def _lru_core_steps(rows, full_ref, g_ref, a_ref, u_ref, h_b, cw, cb, wr_ref, br, wi_ref, bi, lam):
    sp = _softplus(-lam)
    for k in range(LRU_BLOCKS):
        yield
        col = slice(k * LRU_BLOCK_W, (k + 1) * LRU_BLOCK_W)
        xc = _conv_taps(full_ref, rows, cw[:, col], cb[:, col], col)
        xcb = xc.astype(BF16)
        r = jax.nn.sigmoid(_dot(xcb, wr_ref[k]) + br[:, col])
        i = jax.nn.sigmoid(_dot(xcb, wi_ref[k]) + bi[:, col])
        log_a = (-LRU_C * r) * sp[:, col]
        a = jnp.exp(log_a)
        u = jnp.sqrt(-jnp.tanh(log_a) * (a * a + 1.0)) * (i * xc)
        a8, u8 = _scan8(a, u)
        a_ref[:, col] = a8
        u_ref[:, col] = u8

    yield
    for j in range(rows // SUBLANES):
        blk = slice(j * SUBLANES, (j + 1) * SUBLANES)
        hs = u_ref[blk, :] + a_ref[blk, :] * h_b
        u_ref[blk, :] = hs
        h_b = jnp.broadcast_to(hs[SUBLANES - 1:SUBLANES, :], hs.shape)
    return u_ref[...] * g_ref[...], h_b


def _lru_core(*args):
    return _run(_lru_core_steps(*args))


def _lru_post(x, gated, wout_ref, gpost):
    mix = _dot(gated.astype(BF16), wout_ref[...])
    return x + _rmsnorm(mix, gpost)


def _tile_flags(tiles_per_seq):
    step = pl.program_id(0)
    in_seq = lax.rem(step, tiles_per_seq)
    drain = step == pl.num_programs(0) - 1
    return (in_seq == 0) & jnp.logical_not(drain), in_seq == tiles_per_seq - 1, step == 0


def _lru_layer_prompt_kernel(tiles_per_seq, x_ref, gpre_ref, gpost_ref, win_ref, cw_ref, cb_ref, wr_ref, br_ref,
                             wi_ref, bi_ref, lam_ref, wout_ref,
                             fpre_ref, fpost_ref, wg_ref, wu_ref, wd_ref,
                             o_ref, conv_o_ref, h_o_ref,
                             full_ref, g_ref, a_ref, u_ref, h_ref, xmid_ref, act_ref):
    rows = x_ref.shape[0]
    first_tile, last_tile, first_step = _tile_flags(tiles_per_seq)

    @pl.when(first_tile)
    def _():
        full_ref[0:CARRY_ROWS, :] = jnp.zeros((CARRY_ROWS, D_RNN), F32)
        h_ref[...] = jnp.zeros(h_ref.shape, F32)

    @pl.when(first_step)
    def _():
        xmid_ref[...] = jnp.zeros(xmid_ref.shape, F32)

    def mixer():
        x = x_ref[...]
        hn = _rmsnorm(x, gpre_ref[...]).astype(BF16)
        g_ref[...] = _gelu_tanh(_dot(hn, win_ref[:, :D_RNN]))
        yield 1
        full_ref[CARRY_ROWS:CARRY_ROWS + rows, :] = _dot(hn, win_ref[:, D_RNN:])
        core = _lru_core_steps(rows, full_ref, g_ref, a_ref, u_ref, h_ref[...], cw_ref[...], cb_ref[...],
                               wr_ref, br_ref[...], wi_ref, bi_ref[...], lam_ref[...])
        pieces = 0
        try:
            while True:
                next(core)
                pieces += 1
                yield 3 if pieces <= LRU_BLOCKS else 1
        except StopIteration as done:
            gated, h_b = done.value
        h_ref[...] = h_b
        tail = full_ref[rows:rows + CARRY_ROWS, :]
        full_ref[0:CARRY_ROWS, :] = tail
        yield 1
        return _lru_post(x, gated, wout_ref, gpost_ref[...]), tail, h_b

    ffn = _ffn_steps(lambda: xmid_ref[...], fpre_ref[...], fpost_ref[...], wg_ref, wu_ref, wd_ref, act_ref)
    (x_new, tail, h_b), o_ref[...] = _weave(mixer(), ffn, lead=2)
    xmid_ref[...] = x_new

    @pl.when(last_tile)
    def _():
        conv_o_ref[...] = tail
        h_o_ref[...] = h_b[0:1, :]


def _lru_pre_kernel(x_ref, gpre_ref, win_ref, g_o_ref, rec_o_ref):
    g, rec = _lru_pre(x_ref[...], gpre_ref[...], win_ref)
    g_o_ref[...] = g
    rec_o_ref[...] = rec


def _lru_core_sample_body(g_ref, rec_ref, conv_ref, h_ref, cw_ref, cb_ref, wr_ref, br_ref,
                          wi_ref, bi_ref, lam_ref,
                          gated_o_ref, conv_o_ref, h_o_ref,
                          full_ref, a_ref, u_ref):
    rows = rec_ref.shape[0]
    full_ref[0:CARRY_ROWS, :] = jnp.zeros((CARRY_ROWS, D_RNN), F32)
    full_ref[CARRY_LO:CARRY_ROWS, :] = conv_ref[...]
    full_ref[CARRY_ROWS:CARRY_ROWS + rows, :] = rec_ref[...]
    h_b = jnp.broadcast_to(h_ref[...], (SUBLANES, D_RNN))
    gated, h_b = _lru_core(rows, full_ref, g_ref, a_ref, u_ref, h_b, cw_ref[...], cb_ref[...],
                           wr_ref, br_ref[...], wi_ref, bi_ref[...], lam_ref[...])
    gated_o_ref[...] = gated
    conv_o_ref[...] = full_ref[rows:rows + CARRY_ROWS, :]
    h_o_ref[...] = h_b[0:1, :]


def _each_seq(body, refs, n_in, n_const, n_out):
    ins, consts = refs[:n_in], refs[n_in:n_in + n_const]
    outs, scratch = refs[n_in + n_const:n_in + n_const + n_out], refs[n_in + n_const + n_out:]
    for s in range(ins[0].shape[0]):
        body(*[r.at[s] for r in ins], *consts, *[r.at[s] for r in outs], *scratch)


def _lru_core_sample_kernel(*refs):
    _each_seq(_lru_core_sample_body, refs, 4, len(_LRU_CORE), 3)


def _lru_post_ffn_kernel(x_ref, gated_ref, wout_ref, gpost_ref, fpre_ref, fpost_ref,
                         wg_ref, wu_ref, wd_ref, o_ref, act_ref):
    x = _lru_post(x_ref[...], gated_ref[...], wout_ref, gpost_ref[...])
    o_ref[...] = _ffn(x, fpre_ref[...], fpost_ref[...], wg_ref, wu_ref, wd_ref, act_ref)


def _ssd_pre(x, gpre, wzx_ref, dtb, z_ref, xbc_ref, xbc_row0):
    rows = x.shape[0]
    hn = _rmsnorm(x, gpre).astype(BF16)
    for c in range(D_INNER // PROJ_BLK):
        col = slice(c * PROJ_BLK, (c + 1) * PROJ_BLK)
        z_ref[:, col] = _dot(hn, wzx_ref[:, col])
    for c in range(SSD_CONV_DIM // PROJ_BLK):
        col = slice(c * PROJ_BLK, (c + 1) * PROJ_BLK)
        wcol = slice(D_INNER + c * PROJ_BLK, D_INNER + (c + 1) * PROJ_BLK)
        xbc_ref[xbc_row0:xbc_row0 + rows, col] = _dot(hn, wzx_ref[:, wcol])
    return _softplus(_dot(hn, wzx_ref[:, DT_COLS]) + dtb)


def _ssd_conv_silu(full_ref, rows, cw, cb):
    for c in range(SSD_CONV_DIM // PROJ_BLK):
        col = slice(c * PROJ_BLK, (c + 1) * PROJ_BLK)
        out = _conv_taps(full_ref, rows, cw[:, col], cb[:, col], col)
        full_ref[CARRY_ROWS:CARRY_ROWS + rows, col] = _silu(out)


def _cumsum_rows(x):
    rows = x.shape[0]
    row = lax.broadcasted_iota(jnp.int32, x.shape, 0)
    s = 1
    while s < rows:
        x = x + jnp.where(row >= s, pltpu.roll(x, s, 0), 0.0)
        s *= 2
    return x


def _ssd_core_steps(rows, live, full_ref, dt_ref, a_neg, d_exp, st_ref, y_ref, native_state=False):
    q = SSD_Q
    v = live
    tril = (lax.broadcasted_iota(jnp.int32, (v, q), 0) >= lax.broadcasted_iota(jnp.int32, (v, q), 1))
    left = lax.broadcasted_iota(jnp.int32, (v, PAIR_W), 1) < SSD_HEAD_DIM
    left_q = lax.broadcasted_iota(jnp.int32, (q, PAIR_W), 1) < SSD_HEAD_DIM
    contract_last = (((1,), (1,)), ((), ()))
    contract_first = (((0,), (0,)), ((), ()))

    def pad_rows(a):
        if v == q:
            return a
        return jnp.concatenate([a, jnp.zeros((q - v,) + a.shape[1:], a.dtype)], axis=0)

    for c in range(rows // q):
        r0 = c * q
        rf = c * q + CARRY_ROWS
        dtc = dt_ref[pl.ds(r0, v), :]
        acs = _cumsum_rows(dtc * a_neg)
        acs_t = pad_rows(acs).T
        for g in range(SSD_GROUPS):
            yield
            gcol = slice(g * GROUP_W, (g + 1) * GROUP_W)
            bcol = slice(D_INNER + g * SSD_STATE, D_INNER + (g + 1) * SSD_STATE)
            ccol = slice(D_INNER + SSD_GN + g * SSD_STATE, D_INNER + SSD_GN + (g + 1) * SSD_STATE)
            b_g = full_ref[pl.ds(rf, q), bcol].astype(BF16)
            c_g = full_ref[pl.ds(rf, v), ccol].astype(BF16)
            cb = lax.dot_general(c_g, b_g, contract_last, preferred_element_type=F32)
            if native_state:
                y_off = lax.dot_general(c_g, st_ref[gcol, :].astype(BF16), contract_last,
                                        preferred_element_type=F32)
            else:
                y_off = _dot(c_g, st_ref[:, gcol].astype(BF16))
            xdd = []
            dec = []
            for pr in range(SSD_HPG // 2):
                h1 = g * SSD_HPG + 2 * pr
                h2 = h1 + 1
                pcol = slice(h1 * SSD_HEAD_DIM, h1 * SSD_HEAD_DIM + PAIR_W)
                a1 = jnp.broadcast_to(acs[:, h1:h1 + 1], (v, q))
                a2 = jnp.broadcast_to(acs[:, h2:h2 + 1], (v, q))
                l1 = jnp.exp(jnp.where(tril, a1 - acs_t[h1:h1 + 1, :], -jnp.inf))
                l2 = jnp.exp(jnp.where(tril, a2 - acs_t[h2:h2 + 1, :], -jnp.inf))
                m = jnp.concatenate([cb * l1, cb * l2], axis=1).astype(BF16)
                a_pair = jnp.where(left, a1, a2)
                dt_pair = jnp.where(left, jnp.broadcast_to(dtc[:, h1:h1 + 1], (v, PAIR_W)),
                                    jnp.broadcast_to(dtc[:, h2:h2 + 1], (v, PAIR_W)))
                xs = full_ref[pl.ds(rf, v), pcol]
                xdt = xs * dt_pair
                xdt_q = pad_rows(xdt)
                rhs = jnp.concatenate([jnp.where(left_q, xdt_q, 0.0), jnp.where(left_q, 0.0, xdt_q)],
                                      axis=0).astype(BF16)
                y_diag = _dot(m, rhs)
                a_last = a_pair[v - 1:v, :]
                xdd.append(pad_rows((xdt * jnp.exp(a_last - a_pair)).astype(BF16)))
                if native_state:
                    e_last = jnp.exp(a_last)
                    for lane in (0, SSD_HEAD_DIM):
                        dec.append(jnp.broadcast_to(e_last[:, lane:lane + 1], (SSD_HEAD_DIM, SSD_STATE)))
                else:
                    dec.append(jnp.exp(a_last))
                ocol = slice(pr * PAIR_W, (pr + 1) * PAIR_W)
                y_ref[pl.ds(r0, v), pcol] = (y_diag + y_off[:, ocol] * jnp.exp(a_pair)
                                             + d_exp[:, pcol] * xs)
            if native_state:
                states = lax.dot_general(jnp.concatenate(xdd, axis=1), b_g, contract_first,
                                         preferred_element_type=F32)
                st_ref[gcol, :] = st_ref[gcol, :] * jnp.concatenate(dec, axis=0) + states
            else:
                states = lax.dot_general(b_g, jnp.concatenate(xdd, axis=1), contract_first,
                                         preferred_element_type=F32)
                st_ref[:, gcol] = st_ref[:, gcol] * jnp.concatenate(dec, axis=1) + states


def _ssd_core(*args):
    return _run(_ssd_core_steps(*args))


def _ssd_post_steps(x, y_ref, z_ref, normg, wout_ref, gpost, ybf_ref):
    for g in range(SSD_GROUPS):
        yield
        col = slice(g * GROUP_W, (g + 1) * GROUP_W)
        yg = y_ref[:, col] * _silu(z_ref[:, col])
        ms = jnp.mean(yg * yg, axis=-1, keepdims=True)
        ybf_ref[:, col] = (yg * lax.rsqrt(ms + EPS) * normg[:, col]).astype(BF16)
    mix = _dot(ybf_ref[...], wout_ref[...])
    return x + _rmsnorm(mix, gpost)


def _ssd_post(*args):
    return _run(_ssd_post_steps(*args))


def _ssd_layer_prompt_kernel(tiles_per_seq, x_ref, gpre_ref, gpost_ref, wzx_ref, dtb_ref, cw_ref, cb_ref,
                             alog_ref, dexp_ref, normg_ref, wout_ref,
                             fpre_ref, fpost_ref, wg_ref, wu_ref, wd_ref,
                             o_ref, conv_o_ref, st_o_ref,
                             full_ref, z_ref, dt_ref, y_ref, ybf_ref, st_ref, xmid_ref, act_ref):
    rows = x_ref.shape[0]
    first_tile, last_tile, first_step = _tile_flags(tiles_per_seq)

    @pl.when(first_tile)
    def _():
        full_ref[0:CARRY_ROWS, :] = jnp.zeros((CARRY_ROWS, SSD_CONV_DIM), F32)
        st_ref[...] = jnp.zeros(st_ref.shape, F32)

    @pl.when(first_step)
    def _():
        xmid_ref[...] = jnp.zeros(xmid_ref.shape, F32)

    def mixer():
        x = x_ref[...]
        hn = _rmsnorm(x, gpre_ref[...]).astype(BF16)
        dt_ref[...] = _softplus(_dot(hn, wzx_ref[:, DT_COLS]) + dtb_ref[...])
        cw = cw_ref[...]
        cb = cb_ref[...]
        n_blk = SSD_CONV_DIM // PROJ_BLK

        def project(c):
            wcol = slice(D_INNER + c * PROJ_BLK, D_INNER + (c + 1) * PROJ_BLK)
            full_ref[CARRY_ROWS:CARRY_ROWS + rows, c * PROJ_BLK:(c + 1) * PROJ_BLK] = _dot(hn, wzx_ref[:, wcol])

        project(0)
        for c in range(n_blk):
            if c + 1 < n_blk:
                project(c + 1)
            col = slice(c * PROJ_BLK, (c + 1) * PROJ_BLK)
            tail_c = full_ref[rows:rows + CARRY_ROWS, col]
            out = _conv_taps(full_ref, rows, cw[:, col], cb[:, col], col)
            full_ref[CARRY_ROWS:CARRY_ROWS + rows, col] = _silu(out)
            full_ref[0:CARRY_ROWS, col] = tail_c
            yield 1

        z_blocks = iter(range(D_INNER // PROJ_BLK))
        core = _ssd_core_steps(rows, SSD_Q, full_ref, dt_ref, -jnp.exp(alog_ref[...]), dexp_ref[...],
                               st_ref, y_ref)
        for _ in core:
            c = next(z_blocks, None)
            if c is None:
                yield 1
            else:
                col = slice(c * PROJ_BLK, (c + 1) * PROJ_BLK)
                z_ref[:, col] = _dot(hn, wzx_ref[:, col])
                yield c % 2
        post = _ssd_post_steps(x, y_ref, z_ref, normg_ref[...], wout_ref, gpost_ref[...], ybf_ref)
        try:
            while True:
                next(post)
                yield 1
        except StopIteration as done:
            return done.value

    ffn = _ffn_steps(lambda: xmid_ref[...], fpre_ref[...], fpost_ref[...], wg_ref, wu_ref, wd_ref, act_ref)
    x_new, o_ref[...] = _weave(mixer(), ffn, lead=2)
    xmid_ref[...] = x_new

    @pl.when(last_tile)
    def _():
        conv_o_ref[...] = full_ref[0:CARRY_ROWS, :]
        st_o_ref[...] = st_ref[...].T


def _ssd_pre_kernel(x_ref, gpre_ref, wzx_ref, dtb_ref, z_o_ref, xbc_o_ref, dt_o_ref):
    dt_o_ref[...] = _ssd_pre(x_ref[...], gpre_ref[...], wzx_ref, dtb_ref[...],
                             z_o_ref, xbc_o_ref, 0)


def _ssd_core_sample_body(xbc_ref, dt_in_ref, conv_ref, st_in_ref, cw_ref, cb_ref, alog_ref, dexp_ref,
                          y_o_ref, conv_o_ref, st_o_ref,
                          full_ref):
    rows = xbc_ref.shape[0]
    full_ref[...] = jnp.zeros(full_ref.shape, F32)
    full_ref[CARRY_LO:CARRY_ROWS, :] = conv_ref[...]
    full_ref[CARRY_ROWS:CARRY_ROWS + rows, :] = xbc_ref[...]
    conv_o_ref[...] = full_ref[rows:rows + CARRY_ROWS, :]
    _ssd_conv_silu(full_ref, rows, cw_ref[...], cb_ref[...])
    st_o_ref[...] = st_in_ref[...]
    _ssd_core(SSD_Q, rows, full_ref, dt_in_ref, -jnp.exp(alog_ref[...]), dexp_ref[...], st_o_ref, y_o_ref,
              True)


_SSD_CORE_SAMPLE_INS = 8


def _ssd_core_sample_kernel(j, aliased, *refs):
    def real(core_refs):
        _each_seq(_ssd_core_sample_body, core_refs, 4, len(_SSD_CORE), 3)

    if aliased:
        real(refs[:_SSD_CORE_SAMPLE_INS] + refs[_SSD_CORE_SAMPLE_INS + 1:])
        return
    slot = pl.program_id(0)

    @pl.when(slot == j)
    def _():
        real(refs)

    @pl.when(slot != j)
    def _():
        st_o_ref = refs[_SSD_CORE_SAMPLE_INS + 2]
        st_o_ref[...] = jnp.zeros(st_o_ref.shape, F32)


def _ssd_post_ffn_kernel(x_ref, y_ref, z_ref, normg_ref, wout_ref, gpost_ref, fpre_ref, fpost_ref,
                         wg_ref, wu_ref, wd_ref, o_ref, ybf_ref, act_ref):
    x = _ssd_post(x_ref[...], y_ref, z_ref, normg_ref[...], wout_ref, gpost_ref[...], ybf_ref)
    o_ref[...] = _ffn(x, fpre_ref[...], fpost_ref[...], wg_ref, wu_ref, wd_ref, act_ref)


def _layer_block(stacked, idx):
    nd = stacked.ndim
    return pl.BlockSpec((None,) + stacked.shape[1:], lambda *_: (idx,) + (0,) * (nd - 1),
                        pipeline_mode=pl.Buffered(1))


def _pick(params, idx, names):
    return [(params[n], idx) for n in names]


_LRU_MIX = ('gpre', 'gpost', 'win', 'cw', 'cb', 'wr', 'br', 'wi', 'bi', 'lam', 'wout')
_LRU_CORE = ('cw', 'cb', 'wr', 'br', 'wi', 'bi', 'lam')
_SSD_MIX = ('gpre', 'gpost', 'wzx', 'dtb', 'cw', 'cb', 'alog', 'dexp', 'normg', 'wout')
_SSD_CORE = ('cw', 'cb', 'alog', 'dexp')
_FFN = ('gpre', 'gpost', 'wg', 'wu', 'wd')


def _params(n_axes):
    return pltpu.CompilerParams(dimension_semantics=("arbitrary",) * n_axes,
                                vmem_limit_bytes=VMEM_LIMIT)


def _prompt_layer_call(kernel, name, x, t, consts, state_shapes, scratch):
    bsz, seq, _ = x.shape
    nt = seq // t
    n_tiles = bsz * nt

    def mixer_tile(i):
        g = jnp.minimum(i, n_tiles - 1)
        return g // nt, g % nt

    def ffn_tile(i):
        g = jnp.maximum(i - 1, 0)
        return g // nt, g % nt

    return pl.pallas_call(
        functools.partial(kernel, nt),
        grid=(n_tiles + 1,),
        in_specs=[pl.BlockSpec((None, t, D_MODEL), lambda i: (*mixer_tile(i), 0))]
        + [_layer_block(a, k) for a, k in consts],
        out_specs=[pl.BlockSpec((None, t, D_MODEL), lambda i: (*ffn_tile(i), 0))]
        + [pl.BlockSpec((None,) + s, lambda i: (mixer_tile(i)[0], 0, 0)) for s in state_shapes],
        out_shape=[jax.ShapeDtypeStruct((bsz, seq, D_MODEL), F32)]
        + [jax.ShapeDtypeStruct((bsz,) + s, F32) for s in state_shapes],
        scratch_shapes=scratch + [pltpu.VMEM((t, D_MODEL), F32), pltpu.VMEM((t, D_FF), BF16)],
        compiler_params=_params(1),
        name=name,
    )(x, *[a for a, _ in consts])


def _lru_prompt_call(x, lru, j, ffn, layer):
    t = LRU_T
    consts = _pick(lru, j, _LRU_MIX) + _pick(ffn, layer, _FFN)
    return _prompt_layer_call(
        _lru_layer_prompt_kernel, "lru_layer_prompt", x, t, consts,
        [(CARRY_ROWS, D_RNN), (1, D_RNN)],
        [pltpu.VMEM((t + CARRY_ROWS, D_RNN), F32),
         pltpu.VMEM((t, D_RNN), F32),
         pltpu.VMEM((t, D_RNN), F32),
         pltpu.VMEM((t, D_RNN), F32),
         pltpu.VMEM((SUBLANES, D_RNN), F32)])


def _ssd_prompt_call(x, ssd, j, ffn, layer):
    t = SSD_T
    consts = _pick(ssd, j, _SSD_MIX) + _pick(ffn, layer, _FFN)
    return _prompt_layer_call(
        _ssd_layer_prompt_kernel, "ssd_layer_prompt", x, t, consts,
        [(CARRY_ROWS, SSD_CONV_DIM), (D_INNER, SSD_STATE)],
        [pltpu.VMEM((t + CARRY_ROWS, SSD_CONV_DIM), F32),
         pltpu.VMEM((t, D_INNER), F32),
         pltpu.VMEM((t, LANES), F32),
         pltpu.VMEM((t, D_INNER), F32),
         pltpu.VMEM((t, D_INNER), BF16),
         pltpu.VMEM((SSD_STATE, D_INNER), F32)])


def _dense_call(kernel, name, row_ins, consts, out_widths, scratch):
    n = row_ins[0].shape[0]

    def rows_block(width):
        return pl.BlockSpec((n, width), lambda i: (0, 0))

    return pl.pallas_call(
        kernel,
        grid=(1,),
        in_specs=[rows_block(a.shape[1]) for a in row_ins] + [_layer_block(a, k) for a, k in consts],
        out_specs=[rows_block(w) for w in out_widths],
        out_shape=[jax.ShapeDtypeStruct((n, w), F32) for w in out_widths],
        scratch_shapes=scratch,
        compiler_params=_params(1),
        name=name,
    )(*row_ins, *[a for a, _ in consts])


LRU_SAMPLE_G = 4
SSD_SAMPLE_G = 2


def _lru_core_sample_call(g, rec, conv_all, h_all, j, lru):
    bsz, seq, _ = g.shape
    consts = _pick(lru, j, _LRU_CORE)
    grp = LRU_SAMPLE_G

    def per_seq(rows, width):
        return pl.BlockSpec((grp, rows, width), lambda b: (b, 0, 0))

    def per_layer_seq(rows, width):
        return pl.BlockSpec((None, grp, rows, width), lambda b: (j, b, 0, 0))

    return pl.pallas_call(
        _lru_core_sample_kernel,
        grid=(bsz // grp,),
        in_specs=[per_seq(seq, D_RNN), per_seq(seq, D_RNN), per_layer_seq(CONV_WIDTH - 1, D_RNN),
                  per_layer_seq(1, D_RNN)] + [_layer_block(a, k) for a, k in consts],
        out_specs=[per_seq(seq, D_RNN), per_seq(CARRY_ROWS, D_RNN), per_seq(1, D_RNN)],
        out_shape=[jax.ShapeDtypeStruct((bsz, seq, D_RNN), F32),
                   jax.ShapeDtypeStruct((bsz, CARRY_ROWS, D_RNN), F32),
                   jax.ShapeDtypeStruct((bsz, 1, D_RNN), F32)],
        scratch_shapes=[pltpu.VMEM((seq + CARRY_ROWS, D_RNN), F32),
                        pltpu.VMEM((seq, D_RNN), F32),
                        pltpu.VMEM((seq, D_RNN), F32)],
        compiler_params=_params(1),
        name="lru_core_sample",
    )(g, rec, conv_all, h_all, *[a for a, _ in consts])


def _ssd_core_sample_call(xbc, dt, conv_all, st_all, j, st_new_all, ssd):
    bsz, seq, _ = xbc.shape
    consts = _pick(ssd, j, _SSD_CORE)
    aliased = st_new_all is not None
    n_slots = 1 if aliased else st_all.shape[0]
    grp = SSD_SAMPLE_G
    n_grp = bsz // grp

    def seq_of(slot, b):
        if aliased:
            return b
        return jnp.where(slot == j, b, jnp.where(slot < j, 0, n_grp - 1))

    def per_seq(rows, width):
        return pl.BlockSpec((grp, rows, width), lambda slot, b: (seq_of(slot, b), 0, 0))

    def per_layer_seq(rows, width):
        return pl.BlockSpec((None, grp, rows, width), lambda slot, b: (j, seq_of(slot, b), 0, 0))

    out_state = pl.BlockSpec((None, grp, D_INNER, SSD_STATE),
                             lambda slot, b: (j if aliased else slot, b, 0, 0))
    in_specs = [per_seq(seq, SSD_CONV_DIM), per_seq(seq, LANES), per_layer_seq(CONV_WIDTH - 1, SSD_CONV_DIM),
                per_layer_seq(D_INNER, SSD_STATE)] + [_layer_block(a, k) for a, k in consts]
    args = [xbc, dt, conv_all, st_all, *[a for a, _ in consts]]
    assert len(args) == _SSD_CORE_SAMPLE_INS
    aliases = {}
    if aliased:
        in_specs.append(pl.BlockSpec(memory_space=pl.ANY))
        aliases = {len(args): 2}
        args.append(st_new_all)
    return pl.pallas_call(
        functools.partial(_ssd_core_sample_kernel, j, aliased),
        grid=(n_slots, n_grp),
        in_specs=in_specs,
        out_specs=[per_seq(seq, D_INNER), per_seq(CARRY_ROWS, SSD_CONV_DIM), out_state],
        out_shape=[jax.ShapeDtypeStruct((bsz, seq, D_INNER), F32),
                   jax.ShapeDtypeStruct((bsz, CARRY_ROWS, SSD_CONV_DIM), F32),
                   jax.ShapeDtypeStruct(st_all.shape, F32)],
        scratch_shapes=[pltpu.VMEM((SSD_Q + CARRY_ROWS, SSD_CONV_DIM), F32)],
        input_output_aliases=aliases,
        compiler_params=_params(2),
        name="ssd_core_sample",
    )(*args)


def _rows(v):
    return v.reshape(v.shape[0], 1, -1).astype(F32)


def _pad_lanes(v):
    return jnp.pad(v, ((0, 0),) * (v.ndim - 1) + ((0, LANES - v.shape[-1]),))


def kernel(x_prompt, x_sample, state_lru_conv, state_lru_h, state_ssd_conv, state_ssd, norm_mix_pre, norm_mix_post, norm_ffn_pre, norm_ffn_post, lru_w_in, lru_conv_w, lru_conv_b, lru_w_r, lru_b_r, lru_w_i, lru_b_i, lru_lambda, lru_w_out, ssd_w_in, ssd_conv_w, ssd_conv_b, ssd_dt_bias, ssd_a_log, ssd_d, ssd_norm, ssd_w_out, ffn_w_gate, ffn_w_up, ffn_w_down):
    n_a = lru_w_in.shape[0]
    n_b = ssd_w_in.shape[0]
    lru = dict(
        gpre=_rows(norm_mix_pre[0::2]), gpost=_rows(norm_mix_post[0::2]),
        win=lru_w_in.astype(BF16), cw=lru_conv_w, cb=_rows(lru_conv_b),
        wr=lru_w_r.astype(BF16), br=_rows(lru_b_r), wi=lru_w_i.astype(BF16), bi=_rows(lru_b_i),
        lam=_rows(lru_lambda), wout=lru_w_out.astype(BF16))
    ssd = dict(
        gpre=_rows(norm_mix_pre[1::2]), gpost=_rows(norm_mix_post[1::2]),
        wzx=jnp.pad(ssd_w_in, ((0, 0), (0, 0), (0, W_IN_PAD))).astype(BF16),
        dtb=_pad_lanes(_rows(ssd_dt_bias)), cw=ssd_conv_w, cb=_rows(ssd_conv_b),
        alog=_pad_lanes(_rows(ssd_a_log)), dexp=_rows(jnp.repeat(ssd_d, SSD_HEAD_DIM, axis=1)),
        normg=_rows(ssd_norm), wout=ssd_w_out.astype(BF16))
    ffn = dict(
        gpre=_rows(norm_ffn_pre), gpost=_rows(norm_ffn_post),
        wg=ffn_w_gate.astype(BF16), wu=ffn_w_up.astype(BF16), wd=ffn_w_down.astype(BF16))

    bp, seq, _ = x_prompt.shape
    x = x_prompt
    p_lc, p_lh, p_sc, p_ss = [], [], [], []
    for layer in range(DEPTH):
        j = layer // 2
        if layer % 2 == 0:
            x, conv, h = _lru_prompt_call(x, lru, j, ffn, layer)
            p_lc.append(conv[:, CARRY_LO:])
            p_lh.append(h[:, 0])
        else:
            x, conv, st = _ssd_prompt_call(x, ssd, j, ffn, layer)
            p_sc.append(conv[:, CARRY_LO:])
            p_ss.append(st.reshape(bp, SSD_HEADS, SSD_HEAD_DIM, SSD_STATE))
    y_prompt = x

    bs, sl, _ = x_sample.shape
    n = bs * sl
    xs = x_sample.reshape(n, D_MODEL)
    s_lc, s_lh, s_sc = [], [], []
    s_ss_all = None
    for layer in range(DEPTH):
        j = layer // 2
        if layer % 2 == 0:
            g, rec = _dense_call(_lru_pre_kernel, "lru_pre", (xs,), _pick(lru, j, ('gpre', 'win')),
                                 (D_RNN, D_RNN), [])
            gated, conv, h = _lru_core_sample_call(
                g.reshape(bs, sl, D_RNN), rec.reshape(bs, sl, D_RNN), state_lru_conv,
                state_lru_h.reshape(n_a, bs, 1, D_RNN), j, lru)
            s_lc.append(conv[:, CARRY_LO:])
            s_lh.append(h[:, 0])
            (xs,) = _dense_call(
                _lru_post_ffn_kernel, "lru_post_ffn", (xs, gated.reshape(n, D_RNN)),
                _pick(lru, j, ('wout', 'gpost')) + _pick(ffn, layer, _FFN),
                (D_MODEL,), [pltpu.VMEM((n, D_FF), BF16)])
        else:
            z, xbc, dt = _dense_call(_ssd_pre_kernel, "ssd_pre", (xs,),
                                     _pick(ssd, j, ('gpre', 'wzx', 'dtb')),
                                     (D_INNER, SSD_CONV_DIM, LANES), [])
            y, conv, s_ss_all = _ssd_core_sample_call(
                xbc.reshape(bs, sl, SSD_CONV_DIM), dt.reshape(bs, sl, LANES), state_ssd_conv,
                state_ssd.reshape(n_b, bs, D_INNER, SSD_STATE), j, s_ss_all, ssd)
            s_sc.append(conv[:, CARRY_LO:])
            (xs,) = _dense_call(
                _ssd_post_ffn_kernel, "ssd_post_ffn", (xs, y.reshape(n, D_INNER), z),
                _pick(ssd, j, ('normg', 'wout', 'gpost')) + _pick(ffn, layer, _FFN),
                (D_MODEL,), [pltpu.VMEM((n, D_INNER), BF16), pltpu.VMEM((n, D_FF), BF16)])
    y_sample = xs.reshape(bs, sl, D_MODEL)

    return (y_prompt, y_sample,
            jnp.stack(p_lc), jnp.stack(p_lh), jnp.stack(p_sc), jnp.stack(p_ss),
            jnp.stack(s_lc), jnp.stack(s_lh), jnp.stack(s_sc),
            s_ss_all.reshape(n_b, bs, SSD_HEADS, SSD_HEAD_DIM, SSD_STATE))
```
